```python
import math
import jax, jax.numpy as jnp
from jax import lax
import numpy as np

D_MODEL = 4096
BATCH = 4
SEQ = 2048
DEPTH = 2

GRID_W = 64
CTX_LEN = 256
D_FF = 6144
MACARON_W = 0.5
MIX_W = D_MODEL // 4
N_BRANCH = 4
POOL_WINDOWS = (2, 4, 8, 16)
POOL_GROUP = MIX_W // len(POOL_WINDOWS)
CONV_K = 31
NA_HEAD_DIM = 128
NA_HEADS = MIX_W // NA_HEAD_DIM
NA_WIN_ROWS = 8
NA_WIN_COLS = 16
ROPE_BASE = 10000.0
HY_ORDER = 2
HY_SHORT_K = 3
HY_EMB = 33
HY_HIDDEN = 64
HY_FILT_CH = 2 * HY_ORDER * MIX_W
N_ADA = 9
EPS = 1e-6

A_OFF = 0
B_OFF = A_OFF + MIX_W
C_OFF = B_OFF + 2 * MIX_W
D_OFF = C_OFF + 3 * MIX_W
G_OFF = D_OFF + 3 * MIX_W
PROJ_W = G_OFF + N_BRANCH * D_MODEL

kernel_name = 'hybrid_pool_conv_natten_hyena_dit'


def rmsnorm(x, g):
    xf = x.astype(jnp.float32)
    y = xf * lax.rsqrt(jnp.mean(xf * xf, axis=-1, keepdims=True) + EPS)
    return (y * g.astype(jnp.float32)).astype(x.dtype)


def layernorm(x, g, b):
    xf = x.astype(jnp.float32)
    mu = jnp.mean(xf, axis=-1, keepdims=True)
    var = jnp.mean(jnp.square(xf - mu), axis=-1, keepdims=True)
    y = (xf - mu) * lax.rsqrt(var + EPS)
    return (y * g.astype(jnp.float32) + b.astype(jnp.float32)).astype(x.dtype)


def modulate(h, shift, scale):
    return h * (1.0 + scale) + shift


def swiglu(h, w_up, w_down):
    a, b = jnp.split(h @ w_up, 2, axis=-1)
    return (jax.nn.silu(a) * b) @ w_down


def sandwich_ffn(x, m, k, g_pre, g_post, w_up, w_down):
    h = modulate(rmsnorm(x, g_pre), m[:, 3 * k], m[:, 3 * k + 1])
    y = rmsnorm(swiglu(h, w_up, w_down), g_post)
    return x + MACARON_W * m[:, 3 * k + 2] * y


def dwconv(x, w, b):
    K, C = w.shape
    y = lax.conv_general_dilated(x, w[:, None, :].astype(x.dtype), window_strides=(1,),
                                 padding=((K // 2, K // 2),),
                                 dimension_numbers=('NWC', 'WIO', 'NWC'), feature_group_count=C)
    return y + b.astype(x.dtype)


def to_heads(t):
    return t.reshape(t.shape[0], t.shape[1], NA_HEADS, NA_HEAD_DIM)


def pool_mixer(a, w_grp, scale):
    B, L, C = a.shape
    af = a.astype(jnp.float32)
    cs = jnp.concatenate([jnp.zeros((B, 1, C), jnp.float32), jnp.cumsum(af, axis=1)], axis=1)
    t = np.arange(L)
    outs = []
    for g, w in enumerate(POOL_WINDOWS):
        lo = np.clip(t - w // 2, 0, L)
        hi = np.clip(t + w // 2, 0, L)
        sl = slice(g * POOL_GROUP, (g + 1) * POOL_GROUP)
        csg = cs[:, :, sl]
        mean = (csg[:, hi] - csg[:, lo]) / (hi - lo).astype(np.float32)[:, None]
        outs.append(mean - af[:, :, sl])
    y = jnp.stack(outs, axis=2).astype(a.dtype)
    y = jnp.einsum('blgc,gcd->blgd', y, w_grp).reshape(B, L, C)
    return y * scale


def conv_module(u, dw_w, dw_b, ln_g, ln_b):
    a, g = jnp.split(u, 2, axis=-1)
    h = dwconv(a * jax.nn.sigmoid(g), dw_w, dw_b)
    return jax.nn.silu(layernorm(h, ln_g, ln_b))


def axial_rope(x):
    B, L, H, dh = x.shape
    d_axis = dh // 2
    inv = ROPE_BASE ** (-jnp.arange(0, d_axis, 2, dtype=jnp.float32) / d_axis)
    t = jnp.arange(L)
    pos = jnp.stack([t // GRID_W, t % GRID_W], axis=-1).astype(jnp.float32)
    ang = pos[:, :, None] * inv
    cos = jnp.cos(ang)[:, None]
    sin = jnp.sin(ang)[:, None]
    xf = x.astype(jnp.float32).reshape(B, L, H, 2, 2, d_axis // 2)
    x1, x2 = xf[..., 0, :], xf[..., 1, :]
    out = jnp.stack([x1 * cos - x2 * sin, x2 * cos + x1 * sin], axis=-2)
    return out.reshape(x.shape).astype(x.dtype)


def neighbourhood_attention(q, k, v, k_ctx, v_ctx, rpb):
    B, L, H, dh = q.shape
    rows = L // GRID_W
    wr = min(NA_WIN_ROWS, rows)
    scale = dh ** -0.5
    r = np.arange(rows)
    row_idx = np.clip(r - wr // 2, 0, rows - wr)[:, None] + np.arange(wr)[None, :]
    col = np.arange(GRID_W)
    c0 = np.clip(col - NA_WIN_COLS // 2, 0, GRID_W - NA_WIN_COLS)
    col_in = (col[None, :] >= c0[:, None]) & (col[None, :] < c0[:, None] + NA_WIN_COLS)
    qg = axial_rope(q).reshape(B, rows, GRID_W, H, dh)
    kg = axial_rope(k).reshape(B, rows, GRID_W, H, dh)[:, row_idx]
    vg = v.reshape(B, rows, GRID_W, H, dh)[:, row_idx]
    s_loc = jnp.einsum('brchd,briwhd->bhrciw', qg, kg, preferred_element_type=jnp.float32) * scale
    dr = row_idx - r[:, None] + (NA_WIN_ROWS - 1)
    dc = np.clip(col[None, :] - col[:, None] + (NA_WIN_COLS - 1), 0, 2 * NA_WIN_COLS - 2)
    bias = rpb[:, dr[:, None, :, None], dc[None, :, None, :]].astype(jnp.float32)
    s_loc = jnp.where(col_in[:, None, :], s_loc + bias, -jnp.inf)
    q_plain = q.reshape(B, rows, GRID_W, H, dh)
    s_ctx = jnp.einsum('brchd,bnhd->bhrcn', q_plain, k_ctx, preferred_element_type=jnp.float32) * scale
    n_loc = wr * GRID_W
    s = jnp.concatenate([s_loc.reshape(B, H, rows, GRID_W, n_loc), s_ctx], axis=-1)
    p = jax.nn.softmax(s, axis=-1)
    p_loc = p[..., :n_loc].reshape(B, H, rows, GRID_W, wr, GRID_W).astype(v.dtype)
    p_ctx = p[..., n_loc:].astype(v.dtype)
    o = (jnp.einsum('bhrciw,briwhd->brchd', p_loc, vg)
         + jnp.einsum('bhrcn,bnhd->brchd', p_ctx, v_ctx))
    return o.reshape(B, L, H * dh)


def context_attention(q, k, v):
    B, L, H, dh = q.shape
    s = jnp.einsum('bqhd,bkhd->bhqk', q, k, preferred_element_type=jnp.float32) * (dh ** -0.5)
    p = jax.nn.softmax(s, axis=-1).astype(v.dtype)
    return jnp.einsum('bhqk,bkhd->bqhd', p, v).reshape(B, L, H * dh)


def hyena_filters(L, w1, b1, f1, w2, b2, f2, w3, decay):
    f32 = jnp.float32
    t = jnp.linspace(0.0, 1.0, L, dtype=f32)[:, None]
    omega = (2.0 * math.pi / L) * jnp.arange(L, dtype=f32)[:, None]
    bands = (HY_EMB - 1) // 2
    freqs = jnp.linspace(1e-4, bands - 1, bands, dtype=f32)[None, :]
    z = jnp.concatenate([t, jnp.cos(freqs * omega), -jnp.sin(freqs * omega)], axis=-1)
    h = jnp.sin(f1.astype(f32) * (z @ w1.astype(f32) + b1.astype(f32)))
    h = jnp.sin(f2.astype(f32) * (h @ w2.astype(f32) + b2.astype(f32)))
    h = (h @ w3.astype(f32)) * jnp.exp(-t * jnp.abs(decay.astype(f32)))
    return h.reshape(L, 2, HY_ORDER, MIX_W)


def bidir_fftconv(z, h_fwd, h_bwd):
    L, C = h_fwd.shape
    k = jnp.concatenate([h_fwd, jnp.zeros((1, C), jnp.float32), h_bwd[:0:-1]], axis=0)
    kf = jnp.fft.rfft(k, n=2 * L, axis=0)
    zf = jnp.fft.rfft(z.astype(jnp.float32), n=2 * L, axis=1)
    return jnp.fft.irfft(zf * kf[None], n=2 * L, axis=1)[:, :L]


def hyena_mixer(u, short_w, short_b, filt, skip):
    L = u.shape[1]
    u = dwconv(u, short_w, short_b)
    v, x1, x2 = jnp.split(u, 3, axis=-1)
    h = hyena_filters(L, *filt)
    z = v
    for o, gate in enumerate((x1, x2)):
        y = bidir_fftconv(z, h[:, 0, o], h[:, 1, o]) + z.astype(jnp.float32) * skip[o].astype(jnp.float32)
        z = gate * y.astype(gate.dtype)
    return z


def split_projection(p):
    return (p[..., A_OFF:B_OFF], p[..., B_OFF:C_OFF], p[..., C_OFF:D_OFF],
            p[..., D_OFF:G_OFF], p[..., G_OFF:])


def branch_merge(pa, pb, y_attn, pd, pg, pool_w, pool_scale, conv_dw_w, conv_dw_b, conv_ln_g, conv_ln_b,
                 hy_short_w, hy_short_b, hy_filt, hy_bias, w_branch, w_out):
    ys = (pool_mixer(pa, pool_w, pool_scale),
          conv_module(pb, conv_dw_w, conv_dw_b, conv_ln_g, conv_ln_b),
          y_attn,
          hyena_mixer(pd, hy_short_w, hy_short_b, hy_filt, hy_bias))
    gates = jax.nn.sigmoid(pg.reshape(pg.shape[:-1] + (N_BRANCH, D_MODEL)))
    merged = gates[..., 0, :] * (ys[0] @ w_branch[0])
    for i in range(1, N_BRANCH):
        merged = merged + gates[..., i, :] * (ys[i] @ w_branch[i])
    return merged @ w_out


def setup_inputs(seed: int = 0) -> dict:
    key = jax.random.key(seed)
    ks = jax.random.split(key, 40)
    D = D_MODEL

    def nrm(i, shape, scale):
        return jax.random.normal(ks[i], shape, jnp.float32) * scale

    lo_decay = abs(math.log(1e-2)) / 1.5
    hi_decay = abs(math.log(1e-2)) / 0.3
    return {
        'x': nrm(0, (BATCH, SEQ, D), 1.0),
        'c': nrm(1, (BATCH, D), 1.0),
        'ctx': nrm(2, (BATCH, CTX_LEN, D), 1.0),
        'c_ctx': nrm(3, (D,), 1.0),
        'ada_w': nrm(4, (DEPTH, D, N_ADA * D), 0.5 * D ** -0.5),
        'ada_b': nrm(5, (DEPTH, N_ADA * D), 0.02),
        'norm_g': 1.0 + nrm(6, (DEPTH, 6, D), 0.05),
        'ffn1_in': nrm(7, (DEPTH, D, 2 * D_FF), D ** -0.5),
        'ffn1_out': nrm(8, (DEPTH, D_FF, D), D_FF ** -0.5),
        'ffn2_in': nrm(9, (DEPTH, D, 2 * D_FF), D ** -0.5),
        'ffn2_out': nrm(10, (DEPTH, D_FF, D), D_FF ** -0.5),
        'w_in': nrm(11, (DEPTH, D, PROJ_W), D ** -0.5),
        'pool_w': nrm(12, (DEPTH, len(POOL_WINDOWS), POOL_GROUP, POOL_GROUP), POOL_GROUP ** -0.5),
        'pool_scale': 1.0 + nrm(13, (DEPTH, MIX_W), 0.1),
        'conv_dw_w': nrm(14, (DEPTH, CONV_K, MIX_W), CONV_K ** -0.5),
        'conv_dw_b': nrm(15, (DEPTH, MIX_W), 0.02),
        'conv_ln_g': 1.0 + nrm(16, (DEPTH, MIX_W), 0.05),
        'conv_ln_b': nrm(17, (DEPTH, MIX_W), 0.02),
        'na_rpb': nrm(18, (DEPTH, NA_HEADS, 2 * NA_WIN_ROWS - 1, 2 * NA_WIN_COLS - 1), 0.1),
        'hy_short_w': nrm(19, (DEPTH, HY_SHORT_K, 3 * MIX_W), HY_SHORT_K ** -0.5),
        'hy_short_b': nrm(20, (DEPTH, 3 * MIX_W), 0.02),
        'hy_w1': nrm(21, (DEPTH, HY_EMB, HY_HIDDEN), HY_EMB ** -0.5),
        'hy_b1': nrm(22, (DEPTH, HY_HIDDEN), 0.1),
        'hy_freq1': 1.0 + nrm(23, (DEPTH, HY_HIDDEN), 0.1),
        'hy_w2': nrm(24, (DEPTH, HY_HIDDEN, HY_HIDDEN), HY_HIDDEN ** -0.5),
        'hy_b2': nrm(25, (DEPTH, HY_HIDDEN), 0.1),
        'hy_freq2': 1.0 + nrm(26, (DEPTH, HY_HIDDEN), 0.1),
        'hy_w3': nrm(27, (DEPTH, HY_HIDDEN, HY_FILT_CH), 0.1 * HY_HIDDEN ** -0.5),
        'hy_decay': jnp.linspace(lo_decay, hi_decay, HY_FILT_CH, dtype=jnp.float32)[None, :]
                    * (1.0 + nrm(28, (DEPTH, HY_FILT_CH), 0.05)),
        'hy_bias': nrm(29, (DEPTH, HY_ORDER, MIX_W), 0.5),
        'w_branch': nrm(30, (DEPTH, N_BRANCH, MIX_W, D), MIX_W ** -0.5),
        'w_out': nrm(31, (DEPTH, D, D), D ** -0.5),
    }


def reference(x, c, ctx, c_ctx, ada_w, ada_b, norm_g, ffn1_in, ffn1_out, ffn2_in, ffn2_out, w_in,
              pool_w, pool_scale, conv_dw_w, conv_dw_b, conv_ln_g, conv_ln_b, na_rpb,
              hy_short_w, hy_short_b, hy_w1, hy_b1, hy_freq1, hy_w2, hy_b2, hy_freq2, hy_w3, hy_decay,
              hy_bias, w_branch, w_out):
    B = x.shape[0]
    for l in range(DEPTH):
        last = l == DEPTH - 1
        g = norm_g[l]
        m_x = (jax.nn.silu(c) @ ada_w[l] + ada_b[l]).reshape(B, N_ADA, 1, D_MODEL)
        m_c = (jax.nn.silu(c_ctx)[None] @ ada_w[l] + ada_b[l]).reshape(1, N_ADA, 1, D_MODEL)

        x = sandwich_ffn(x, m_x, 0, g[0], g[1], ffn1_in[l], ffn1_out[l])
        ctx = sandwich_ffn(ctx, m_c, 0, g[0], g[1], ffn1_in[l], ffn1_out[l])

        mix_params = (pool_w[l], pool_scale[l], conv_dw_w[l], conv_dw_b[l], conv_ln_g[l], conv_ln_b[l],
                      hy_short_w[l], hy_short_b[l],
                      (hy_w1[l], hy_b1[l], hy_freq1[l], hy_w2[l], hy_b2[l], hy_freq2[l], hy_w3[l], hy_decay[l]),
                      hy_bias[l], w_branch[l], w_out[l])
        hx = modulate(rmsnorm(x, g[2]), m_x[:, 3], m_x[:, 4])
        hc = modulate(rmsnorm(ctx, g[2]), m_c[:, 3], m_c[:, 4])
        if not last:
            pa_c, pb_c, pqkv_c, pd_c, pg_c = split_projection(hc @ w_in[l])
            q_c, k_c, v_c = jnp.split(pqkv_c, 3, axis=-1)
            q_c, k_c, v_c = to_heads(q_c), to_heads(k_c), to_heads(v_c)
            y_c = branch_merge(pa_c, pb_c, context_attention(q_c, k_c, v_c), pd_c, pg_c, *mix_params)
            ctx = ctx + m_c[:, 5] * rmsnorm(y_c, g[3])
        else:
            kv_c = hc @ w_in[l][:, C_OFF + MIX_W:D_OFF]
            k_c, v_c = jnp.split(kv_c, 2, axis=-1)
            k_c, v_c = to_heads(k_c), to_heads(v_c)
        pa, pb, pqkv, pd, pg = split_projection(hx @ w_in[l])
        q, k, v = jnp.split(pqkv, 3, axis=-1)
        y_attn = neighbourhood_attention(to_heads(q), to_heads(k), to_heads(v), k_c, v_c, na_rpb[l])
        y_x = branch_merge(pa, pb, y_attn, pd, pg, *mix_params)
        x = x + m_x[:, 5] * rmsnorm(y_x, g[3])

        x = sandwich_ffn(x, m_x, 2, g[4], g[5], ffn2_in[l], ffn2_out[l])
        if not last:
            ctx = sandwich_ffn(ctx, m_c, 2, g[4], g[5], ffn2_in[l], ffn2_out[l])
    return x
```

```python
import functools
import math

import numpy as np
import jax
import jax.numpy as jnp
from jax import lax
from jax.experimental import pallas as pl
from jax.experimental.pallas import tpu as pltpu

F32 = jnp.float32
BF16 = jnp.bfloat16

D = 4096
BATCH = 4
SEQ = 2048
DEPTH = 2
GRID_W = 64
GRID_H = SEQ // GRID_W
CTX = 256
D_FF = 6144
MIX = D // 4
N_BRANCH = 4
POOL_WINDOWS = (2, 4, 8, 16)
POOL_G = MIX // len(POOL_WINDOWS)
CONV_K = 31
DH = 128
HEADS = MIX // DH
WIN_R = 8
WIN_C = 16
ROPE_BASE = 10000.0
HY_EMB = 33
HY_HID = 64
N_ADA = 9
EPS = 1e-6
PROJ_W = 9 * MIX + N_BRANCH * D
G_OFF = 9 * MIX

N_LAT = BATCH * SEQ
N_CTX = BATCH * CTX
N_ALL = N_LAT + N_CTX

P_DT = BF16
NEG = -1e30

ATT_RQ = 4
ATT_KR = 12
MIB = 1024 * 1024


def _cp(sem, vmem_mib):
    return pltpu.CompilerParams(dimension_semantics=sem, vmem_limit_bytes=vmem_mib * MIB)


def _modrow(i, bm):
    return jnp.minimum((i * bm) // SEQ, BATCH)


def _silu(v):
    return v * jax.nn.sigmoid(v)


def _ada_kernel(c_ref, w_ref, b_ref, o_ref):
    s = _silu(c_ref[...]).astype(BF16)
    o_ref[...] = jnp.dot(s, w_ref[...].astype(BF16), preferred_element_type=F32) + b_ref[...]


def _ada(cc, ada_w, ada_b3, l):
    bn = 512
    return pl.pallas_call(
        _ada_kernel,
        out_shape=jax.ShapeDtypeStruct((8, N_ADA * D), F32),
        grid=(N_ADA * D // bn,),
        in_specs=[pl.BlockSpec((8, D), lambda j: (0, 0)),
                  pl.BlockSpec((None, D, bn), lambda j: (l, 0, j)),
                  pl.BlockSpec((None, 1, bn), lambda j: (l, 0, j))],
        out_specs=pl.BlockSpec((8, bn), lambda j: (0, j)),
        compiler_params=_cp(("arbitrary",), 40),
    )(cc, ada_w, ada_b3)


def _rms(v, g):
    return v * lax.rsqrt(jnp.mean(v * v, axis=-1, keepdims=True) + EPS) * g


def _normmod_kernel(x_ref, g_ref, sh_ref, sc_ref, h_ref):
    y = _rms(x_ref[...], g_ref[...])
    h_ref[...] = (y * (1.0 + sc_ref[...]) + sh_ref[...]).astype(h_ref.dtype)


def _mod_spec(bm, k):
    return pl.BlockSpec((None, 1, D), lambda i: (_modrow(i, bm), 0, k))


def _g_spec(l, k):
    return pl.BlockSpec((None, 1, D), lambda i: (l * 6 + k, 0, 0))


def _normmod(x, g3, m3, l, gk, mk, rows):
    bm = 256
    return pl.pallas_call(
        _normmod_kernel,
        out_shape=jax.ShapeDtypeStruct((rows, D), BF16),
        grid=(rows // bm,),
        in_specs=[pl.BlockSpec((bm, D), lambda i: (i, 0)), _g_spec(l, gk),
                  _mod_spec(bm, mk), _mod_spec(bm, mk + 1)],
        out_specs=pl.BlockSpec((bm, D), lambda i: (i, 0)),
        compiler_params=_cp(("arbitrary",), 32),
    )(x, g3, m3, m3)


def _resid_kernel(x_ref, y_ref, gp_ref, gate_ref, *rest, coef, with_next):
    xn = x_ref[...] + coef * gate_ref[...] * _rms(y_ref[...].astype(F32), gp_ref[...])
    if with_next:
        gn_ref, sh_ref, sc_ref, xo_ref, h_ref = rest
        xo_ref[...] = xn
        h_ref[...] = (_rms(xn, gn_ref[...]) * (1.0 + sc_ref[...]) + sh_ref[...]).astype(h_ref.dtype)
    else:
        (xo_ref,) = rest
        xo_ref[...] = xn


def _resid(x, y, g3, m3, l, gk_post, mk_gate, coef, rows, nxt=None):
    bm = 256
    in_specs = [pl.BlockSpec((bm, D), lambda i: (i, 0)), pl.BlockSpec((bm, D), lambda i: (i, 0)),
                _g_spec(l, gk_post), _mod_spec(bm, mk_gate)]
    args = [x, y, g3, m3]
    out_shape = [jax.ShapeDtypeStruct((rows, D), F32)]
    out_specs = [pl.BlockSpec((bm, D), lambda i: (i, 0))]
    if nxt is not None:
        g3n, m3n, ln, gkn, mkn = nxt
        in_specs += [_g_spec(ln, gkn), _mod_spec(bm, mkn), _mod_spec(bm, mkn + 1)]
        args += [g3n, m3n, m3n]
        out_shape.append(jax.ShapeDtypeStruct((rows, D), BF16))
        out_specs.append(pl.BlockSpec((bm, D), lambda i: (i, 0)))
    res = pl.pallas_call(
        functools.partial(_resid_kernel, coef=coef, with_next=nxt is not None),
        out_shape=out_shape, grid=(rows // bm,), in_specs=in_specs, out_specs=out_specs,
        compiler_params=_cp(("arbitrary",), 48),
    )(*args)
    return res if nxt is not None else res[0]


def _mm_kernel(x_ref, w_ref, o_ref, wbf_ref):
    @pl.when(pl.program_id(1) == 0)
    def _():
        wbf_ref[...] = w_ref[...].astype(BF16)

    o_ref[...] = jnp.dot(x_ref[...], wbf_ref[...], preferred_element_type=F32).astype(o_ref.dtype)


def _matmul(x, w, l, rows, n_out, out_dtype, *, bm, bn, row_off=0, col_off=0, vmem=48):
    K = x.shape[1]
    ro, co = row_off // bm, col_off // bn
    return pl.pallas_call(
        _mm_kernel,
        out_shape=jax.ShapeDtypeStruct((rows, n_out), out_dtype),
        grid=(n_out // bn, rows // bm),
        in_specs=[pl.BlockSpec((bm, K), lambda j, i: (i + ro, 0)),
                  pl.BlockSpec((None, K, bn), lambda j, i: (l, 0, j + co))],
        out_specs=pl.BlockSpec((bm, bn), lambda j, i: (i, j)),
        scratch_shapes=[pltpu.VMEM((K, bn), BF16)],
        compiler_params=_cp(("arbitrary", "arbitrary"), vmem),
    )(x, w)


def _ffn_up_kernel(x_ref, wa_ref, wb_ref, o_ref, wa_bf, wb_bf):
    @pl.when(pl.program_id(1) == 0)
    def _():
        wa_bf[...] = wa_ref[...].astype(BF16)
        wb_bf[...] = wb_ref[...].astype(BF16)

    x = x_ref[...]
    a = jnp.dot(x, wa_bf[...], preferred_element_type=F32)
    b = jnp.dot(x, wb_bf[...], preferred_element_type=F32)
    o_ref[...] = (_silu(a) * b).astype(o_ref.dtype)


def _ffn_up(h, w_up, l, rows):
    bm, bn = 1024, 256
    nb = D_FF // bn
    return pl.pallas_call(
        _ffn_up_kernel,
        out_shape=jax.ShapeDtypeStruct((rows, D_FF), BF16),
        grid=(nb, rows // bm),
        in_specs=[pl.BlockSpec((bm, D), lambda j, i: (i, 0)),
                  pl.BlockSpec((None, D, bn), lambda j, i: (l, 0, j)),
                  pl.BlockSpec((None, D, bn), lambda j, i: (l, 0, j + nb))],
        out_specs=pl.BlockSpec((bm, bn), lambda j, i: (i, j)),
        scratch_shapes=[pltpu.VMEM((D, bn), BF16), pltpu.VMEM((D, bn), BF16)],
        compiler_params=_cp(("arbitrary", "arbitrary"), 48),
    )(h, w_up, w_up)


def _branch_kernel(y0, y1, y2, y3, g0, g1, g2, g3, w_ref, o_ref, wbf_ref):
    @pl.when(pl.program_id(1) == 0)
    def _():
        wbf_ref[...] = w_ref[...].astype(BF16)

    acc = None
    for i, (y, g) in enumerate(((y0, g0), (y1, g1), (y2, g2), (y3, g3))):
        t = jax.nn.sigmoid(g[...].astype(F32)) * jnp.dot(y[...], wbf_ref[i], preferred_element_type=F32)
        acc = t if acc is None else acc + t
    o_ref[...] = acc.astype(o_ref.dtype)


def _branch(ys, p, w_branch, l, rows):
    bm, bn = 512, 512
    gate_specs = [pl.BlockSpec((bm, bn), functools.partial(lambda j, i, br: (i, (G_OFF + br * D) // bn + j), br=br))
                  for br in range(N_BRANCH)]
    return pl.pallas_call(
        _branch_kernel,
        out_shape=jax.ShapeDtypeStruct((rows, D), BF16),
        grid=(D // bn, rows // bm),
        in_specs=[pl.BlockSpec((bm, MIX), lambda j, i: (i, 0))] * N_BRANCH + gate_specs
                 + [pl.BlockSpec((None, N_BRANCH, MIX, bn), lambda j, i: (l, 0, 0, j))],
        out_specs=pl.BlockSpec((bm, bn), lambda j, i: (i, j)),
        scratch_shapes=[pltpu.VMEM((N_BRANCH, MIX, bn), BF16)],
        compiler_params=_cp(("arbitrary", "arbitrary"), 48),
    )(*ys, p, p, p, p, w_branch)


def _pool_bands(S):
    t = jnp.arange(S, dtype=jnp.int32)[:, None]
    m = jnp.arange(S, dtype=jnp.int32)[None, :]
    bands = []
    for w in POOL_WINDOWS:
        lo = jnp.clip(t - w // 2, 0, S)
        hi = jnp.clip(t + w // 2, 0, S)
        inside = (m >= lo) & (m < hi)
        band = jnp.where(inside, 1.0 / (hi - lo).astype(F32), 0.0) - (m == t).astype(F32)
        bands.append(band)
    return jnp.stack(bands).astype(BF16)


def _pool_kernel(band_ref, p_ref, w_ref, sc_ref, o_ref):
    y = jnp.dot(band_ref[...], p_ref[...].astype(BF16), preferred_element_type=F32)
    z = jnp.dot(y.astype(BF16), w_ref[...].astype(BF16), preferred_element_type=F32)
    o_ref[...] = (z * sc_ref[...]).astype(o_ref.dtype)


def _pool(p, bands, pool_w, pool_scale3, l, S, nseq, row_off):
    ro = row_off // S
    ng = len(POOL_WINDOWS)
    return pl.pallas_call(
        _pool_kernel,
        out_shape=jax.ShapeDtypeStruct((nseq * S, MIX), BF16),
        grid=(ng, nseq),
        in_specs=[pl.BlockSpec((None, S, S), lambda g, s: (g, 0, 0)),
                  pl.BlockSpec((S, POOL_G), lambda g, s: (s + ro, g)),
                  pl.BlockSpec((None, None, POOL_G, POOL_G), lambda g, s: (l, g, 0, 0)),
                  pl.BlockSpec((None, 1, POOL_G), lambda g, s: (l, 0, g))],
        out_specs=pl.BlockSpec((S, POOL_G), lambda g, s: (s, g)),
        compiler_params=_cp(("arbitrary", "arbitrary"), 40),
    )(bands, p, pool_w, pool_scale3)


CONV_PAD = 16
CONV_CH = 32


def _conv_kernel(a_ref, g_ref, w_ref, b_ref, lg_ref, lb_ref, o_ref, u_ref, *, S):
    C = a_ref.shape[1]
    zeros = jnp.zeros((CONV_PAD, C), F32)
    u_ref[0:CONV_PAD, :] = zeros
    u_ref[S + CONV_PAD:S + 2 * CONV_PAD, :] = zeros
    SC = 128

    def stage(i, carry):
        r0 = pl.multiple_of(i * SC, SC)
        a = a_ref[pl.ds(r0, SC), :].astype(F32)
        g = g_ref[pl.ds(r0, SC), :].astype(F32)
        u_ref[pl.ds(r0 + CONV_PAD, SC), :] = a * jax.nn.sigmoid(g)
        return carry

    lax.fori_loop(0, S // SC, stage, 0)

    n = CONV_CH + 2 * CONV_PAD
    off = CONV_PAD - CONV_K // 2

    def body(i, carry):
        r0 = pl.multiple_of(i * CONV_CH, CONV_CH)
        win = u_ref[pl.ds(r0, n), :]
        acc = jnp.broadcast_to(b_ref[...], (CONV_CH, C))
        for sub in range(8):
            wsub = win if sub == 0 else pltpu.roll(win, n - sub, axis=0)
            for al in range(n // 8):
                k = 8 * al + sub - off
                if 0 <= k < CONV_K:
                    acc = acc + w_ref[k:k + 1, :] * wsub[8 * al:8 * al + CONV_CH]
        mu = jnp.mean(acc, axis=-1, keepdims=True)
        xc = acc - mu
        var = jnp.mean(xc * xc, axis=-1, keepdims=True)
        y = xc * lax.rsqrt(var + EPS) * lg_ref[...] + lb_ref[...]
        o_ref[pl.ds(r0, CONV_CH), :] = _silu(y).astype(o_ref.dtype)
        return carry

    lax.fori_loop(0, S // CONV_CH, body, 0)


def _conv(p, dw_w, dw_b3, ln_g3, ln_b3, l, S, nseq, row_off):
    ro = row_off // S
    vec = lambda: pl.BlockSpec((None, 1, MIX), lambda s: (l, 0, 0))
    return pl.pallas_call(
        functools.partial(_conv_kernel, S=S),
        out_shape=jax.ShapeDtypeStruct((nseq * S, MIX), BF16),
        grid=(nseq,),
        in_specs=[pl.BlockSpec((S, MIX), lambda s: (s + ro, 1)),
                  pl.BlockSpec((S, MIX), lambda s: (s + ro, 2)),
                  pl.BlockSpec((None, CONV_K, MIX), lambda s: (l, 0, 0)),
                  vec(), vec(), vec()],
        out_specs=pl.BlockSpec((S, MIX), lambda s: (s, 0)),
        scratch_shapes=[pltpu.VMEM((S + 2 * CONV_PAD, MIX), F32)],
        compiler_params=_cp(("arbitrary",), 48),
    )(p, p, dw_w, dw_b3, ln_g3, ln_b3)


HY_PAD = 8
HY_CH = 64


def _hyshort_kernel(p_ref, w_ref, b_ref, o_ref, u_ref, *, S):
    C = p_ref.shape[1]
    zeros = jnp.zeros((HY_PAD, C), F32)
    u_ref[0:HY_PAD, :] = zeros
    u_ref[S + HY_PAD:S + 2 * HY_PAD, :] = zeros
    SC = 128

    def stage(i, carry):
        r0 = pl.multiple_of(i * SC, SC)
        u_ref[pl.ds(r0 + HY_PAD, SC), :] = p_ref[pl.ds(r0, SC), :].astype(F32)
        return carry

    lax.fori_loop(0, S // SC, stage, 0)
    n = HY_CH + 2 * HY_PAD

    def body(i, carry):
        r0 = pl.multiple_of(i * HY_CH, HY_CH)
        win = u_ref[pl.ds(r0, n), :]
        prev = pltpu.roll(win, n - (HY_PAD - 1), axis=0)[0:HY_CH]
        mid = win[HY_PAD:HY_PAD + HY_CH]
        nxt = pltpu.roll(win, n - 1, axis=0)[HY_PAD:HY_PAD + HY_CH]
        o_ref[pl.ds(r0, HY_CH), :] = (w_ref[0:1, :] * prev + w_ref[1:2, :] * mid + w_ref[2:3, :] * nxt
                                       + b_ref[...])
        return carry

    lax.fori_loop(0, S // HY_CH, body, 0)


def _hyshort(p, sw, sb3, l, S, nseq, row_off):
    ro = row_off // S
    cb = (6 * MIX) // MIX
    return pl.pallas_call(
        functools.partial(_hyshort_kernel, S=S),
        out_shape=jax.ShapeDtypeStruct((nseq * S, 3 * MIX), F32),
        grid=(nseq, 3),
        in_specs=[pl.BlockSpec((S, MIX), lambda s, c: (s + ro, cb + c)),
                  pl.BlockSpec((None, 3, MIX), lambda s, c: (l, 0, c)),
                  pl.BlockSpec((None, 1, MIX), lambda s, c: (l, 0, c))],
        out_specs=pl.BlockSpec((S, MIX), lambda s, c: (s, c)),
        scratch_shapes=[pltpu.VMEM((S + 2 * HY_PAD, MIX), F32)],
        compiler_params=_cp(("arbitrary", "arbitrary"), 48),
    )(p, sw, sb3)


def _hy_feats(L):
    t = jnp.linspace(0.0, 1.0, L, dtype=F32)[:, None]
    omega = (2.0 * math.pi / L) * jnp.arange(L, dtype=F32)[:, None]
    bands = (HY_EMB - 1) // 2
    freqs = jnp.linspace(1e-4, bands - 1, bands, dtype=F32)[None, :]
    z = jnp.concatenate([t, jnp.cos(freqs * omega), -jnp.sin(freqs * omega)], axis=-1)
    return jnp.pad(z, ((0, 0), (0, 128 - HY_EMB)))


def _hyfilt_kernel(z_ref, w1_ref, b1_ref, f1_ref, w2_ref, b2_ref, f2_ref, w3_ref, dec_ref, o_ref, *, L):
    CH = 256 if L >= 256 else L
    hp = lax.Precision.HIGHEST
    C = o_ref.shape[1]

    def body(i, carry):
        r0 = pl.multiple_of(i * CH, CH)
        z = z_ref[pl.ds(r0, CH), :]
        h = jnp.sin(f1_ref[...] * (jnp.dot(z, w1_ref[...], precision=hp, preferred_element_type=F32) + b1_ref[...]))
        h = jnp.sin(f2_ref[...] * (jnp.dot(h, w2_ref[...], precision=hp, preferred_element_type=F32) + b2_ref[...]))
        o = jnp.dot(h, w3_ref[...], precision=hp, preferred_element_type=F32)
        t = (lax.broadcasted_iota(jnp.int32, (CH, C), 0) + r0).astype(F32) * (1.0 / (L - 1))
        o_ref[pl.ds(r0, CH), :] = o * jnp.exp(-t * jnp.abs(dec_ref[...]))
        return carry

    lax.fori_loop(0, L // CH, body, 0)


def _hyfilt(zf, w1p, b1p, f1p, w2p, b2p, f2p, w3p, dec3, l, L):
    bn = 1024
    nfc = 4 * MIX
    small = lambda shp: pl.BlockSpec(shp, lambda j: (0,) * len(shp))
    return pl.pallas_call(
        functools.partial(_hyfilt_kernel, L=L),
        out_shape=jax.ShapeDtypeStruct((L, nfc), F32),
        grid=(nfc // bn,),
        in_specs=[small((L, 128)), small((128, 128)), small((1, 128)), small((1, 128)),
                  small((128, 128)), small((1, 128)), small((1, 128)),
                  pl.BlockSpec((128, bn), lambda j: (0, j)),
                  pl.BlockSpec((None, 1, bn), lambda j: (l, 0, j))],
        out_specs=pl.BlockSpec((L, bn), lambda j: (0, j)),
        compiler_params=_cp(("arbitrary",), 40),
    )(zf, w1p, b1p, f1p, w2p, b2p, f2p, w3p, dec3)


def _dft_mats(L):
    N = 2 * L
    a = jnp.arange(L, dtype=jnp.int32)
    idx = (a[:, None] * a[None, :]) & (N - 1)
    ang = idx.astype(F32) * (2.0 * math.pi / N)
    cs, sn = jnp.cos(ang), jnp.sin(ang)
    alt = (1 - 2 * (a & 1)).astype(F32)
    first = (a == 0)
    fs = jnp.where(first[:, None], alt[None, :], -sn)
    F = jnp.stack([cs, fs]).astype(BF16)
    wf = jnp.where(first, 1.0 / N, 2.0 / N).astype(F32)
    gc = cs * wf[None, :]
    gs = jnp.where(first[None, :], alt[:, None] / N, -sn * (2.0 / N))
    G = jnp.concatenate([gc, gs], axis=1).astype(BF16)
    return F, G


def _kf_kernel(fc_ref, fs_ref, hf_ref, hb_ref, o_ref, hs_s, hf_s, hb_s):
    @pl.when(pl.program_id(2) == 0)
    def _():
        hf = hf_ref[...]
        rows = lax.broadcasted_iota(jnp.int32, hf.shape, 0)
        hb = jnp.where(rows == 0, 0.0, hb_ref[...])
        hs_s[...] = (hf + hb).astype(BF16)
        hf_s[...] = hf.astype(BF16)
        hb_s[...] = hb.astype(BF16)

    kr = jnp.dot(fc_ref[...], hs_s[...], preferred_element_type=F32)
    ia = jnp.dot(fs_ref[...], hf_s[...], preferred_element_type=F32)
    ib = jnp.dot(fs_ref[...], hb_s[...], preferred_element_type=F32)
    bf = kr.shape[0]
    frow = lax.broadcasted_iota(jnp.int32, kr.shape, 0) + pl.program_id(2) * bf
    o_ref[0] = kr
    o_ref[1] = jnp.where(frow == 0, ia + ib, ia - ib)


def _kf(F, hfilt, L):
    bf = min(512, L)
    cw = 512
    ncb = MIX // cw
    return pl.pallas_call(
        _kf_kernel,
        out_shape=jax.ShapeDtypeStruct((2, 2, L, MIX), F32),
        grid=(2, ncb, L // bf),
        in_specs=[pl.BlockSpec((None, bf, L), lambda o, c, f: (0, f, 0)),
                  pl.BlockSpec((None, bf, L), lambda o, c, f: (1, f, 0)),
                  pl.BlockSpec((L, cw), lambda o, c, f: (0, o * ncb + c)),
                  pl.BlockSpec((L, cw), lambda o, c, f: (0, (2 + o) * ncb + c))],
        out_specs=pl.BlockSpec((None, 2, bf, cw), lambda o, c, f: (o, 0, f, c)),
        scratch_shapes=[pltpu.VMEM((L, cw), BF16)] * 3,
        compiler_params=_cp(("arbitrary", "arbitrary", "arbitrary"), 48),
    )(F, F, hfilt, hfilt)


def _hyfwd_kernel(z_ref, fc_ref, fs_ref, kr_ref, ki_ref, o_ref, zbf_ref):
    @pl.when(pl.program_id(1) == 0)
    def _():
        zbf_ref[...] = z_ref[...].astype(BF16)

    zb = zbf_ref[...]
    zr = jnp.dot(fc_ref[...], zb, preferred_element_type=F32)
    zi = jnp.dot(fs_ref[...], zb, preferred_element_type=F32)
    kr, ki = kr_ref[...], ki_ref[...]
    bf = zr.shape[0]
    frow = lax.broadcasted_iota(jnp.int32, zr.shape, 0) + pl.program_id(1) * bf
    dc = frow == 0
    o_ref[0] = jnp.where(dc, zr * kr, zr * kr - zi * ki).astype(o_ref.dtype)
    o_ref[1] = jnp.where(dc, zi * ki, zr * ki + zi * kr).astype(o_ref.dtype)


def _hyfwd(z, zcol, F, KF, o, S, nseq):
    bf = min(512, S)
    return pl.pallas_call(
        _hyfwd_kernel,
        out_shape=jax.ShapeDtypeStruct((nseq, 2, S, MIX), BF16),
        grid=(nseq, S // bf),
        in_specs=[pl.BlockSpec((S, MIX), lambda s, f: (s, zcol)),
                  pl.BlockSpec((None, bf, S), lambda s, f: (0, f, 0)),
                  pl.BlockSpec((None, bf, S), lambda s, f: (1, f, 0)),
                  pl.BlockSpec((None, None, bf, MIX), lambda s, f: (o, 0, f, 0)),
                  pl.BlockSpec((None, None, bf, MIX), lambda s, f: (o, 1, f, 0))],
        out_specs=pl.BlockSpec((None, 2, bf, MIX), lambda s, f: (s, 0, f, 0)),
        scratch_shapes=[pltpu.VMEM((S, MIX), BF16)],
        compiler_params=_cp(("arbitrary", "arbitrary"), 48),
    )(z, F, F, KF, KF)


def _hyinv_kernel(g_ref, p_ref, z_ref, gate_ref, skip_ref, o_ref):
    y = jnp.dot(g_ref[...], p_ref[...], preferred_element_type=F32)
    y = y + z_ref[...] * skip_ref[...]
    o_ref[...] = (gate_ref[...] * y).astype(o_ref.dtype)


def _hyinv(G, P, z, zcol, u3, gcol, hy_bias, l, o, S, nseq, out_dtype):
    bn = min(512, S)
    nb = S // bn
    P2 = P.reshape(nseq, 2 * S, MIX)
    return pl.pallas_call(
        _hyinv_kernel,
        out_shape=jax.ShapeDtypeStruct((nseq * S, MIX), out_dtype),
        grid=(nseq, nb),
        in_specs=[pl.BlockSpec((bn, 2 * S), lambda s, n: (n, 0)),
                  pl.BlockSpec((None, 2 * S, MIX), lambda s, n: (s, 0, 0)),
                  pl.BlockSpec((bn, MIX), lambda s, n: (s * nb + n, zcol)),
                  pl.BlockSpec((bn, MIX), lambda s, n: (s * nb + n, gcol)),
                  pl.BlockSpec((None, None, 1, MIX), lambda s, n: (l, o, 0, 0))],
        out_specs=pl.BlockSpec((bn, MIX), lambda s, n: (s * nb + n, 0)),
        compiler_params=_cp(("arbitrary", "arbitrary"), 48),
    )(G, P2, z, u3, hy_bias)


def _hyena(p, F, G, KF, hy_short_w, hy_short_b3, hy_bias4, l, S, nseq, row_off):
    u3 = _hyshort(p, hy_short_w, hy_short_b3, l, S, nseq, row_off)
    P = _hyfwd(u3, 0, F, KF, 0, S, nseq)
    z1 = _hyinv(G, P, u3, 0, u3, 1, hy_bias4, l, 0, S, nseq, F32)
    P = _hyfwd(z1, 0, F, KF, 1, S, nseq)
    return _hyinv(G, P, z1, 0, u3, 2, hy_bias4, l, 1, S, nseq, BF16)


def _rope_tables():
    d_axis = DH // 2
    inv = ROPE_BASE ** (-jnp.arange(0, d_axis, 2, dtype=F32) / d_axis)
    t = jnp.arange(SEQ)
    pos = jnp.stack([t // GRID_W, t % GRID_W], axis=-1).astype(F32)
    lane = np.arange(DH)
    ang = pos[:, lane // d_axis] * inv[lane % (d_axis // 2)][None, :]
    sign = np.where((lane % d_axis) < d_axis // 2, -1.0, 1.0).astype(np.float32)
    return jnp.cos(ang), jnp.sin(ang) * sign[None, :]


def _attn_bias(rpb_l):
    nblk = GRID_H // ATT_RQ
    col = np.arange(GRID_W)
    c0 = np.clip(col - WIN_C // 2, 0, GRID_W - WIN_C)
    col_in = (col[None, :] >= c0[:, None]) & (col[None, :] < c0[:, None] + WIN_C)
    dc = np.clip(col[None, :] - col[:, None] + (WIN_C - 1), 0, 2 * WIN_C - 2)
    out = []
    for blk in (0, 1, nblk - 1):
        start0 = int(np.clip(ATT_RQ * blk - WIN_R // 2, 0, GRID_H - ATT_KR))
        r = ATT_RQ * blk + np.arange(ATT_RQ)
        rs = np.clip(r - WIN_R // 2, 0, GRID_H - WIN_R)
        kr = start0 + np.arange(ATT_KR)
        row_in = (kr[None, :] >= rs[:, None]) & (kr[None, :] < rs[:, None] + WIN_R)
        dr = np.clip(kr[None, :] - r[:, None] + (WIN_R - 1), 0, 2 * WIN_R - 2)
        b = rpb_l[:, dr[:, None, :, None], dc[None, :, None, :]].astype(F32)
        ok = row_in[:, None, :, None] & col_in[None, :, None, :]
        b = jnp.where(ok[None], b, NEG)
        out.append(b.reshape(HEADS, ATT_RQ * GRID_W, ATT_KR * GRID_W))
    return jnp.stack(out, axis=1)


def _rope(x, cos, sin_signed):
    lane = lax.broadcasted_iota(jnp.int32, x.shape, 1)
    partner = jnp.where((lane & 32) == 0, pltpu.roll(x, DH - 32, axis=1), pltpu.roll(x, 32, axis=1))
    return x * cos + partner * sin_signed


def _attn_kernel(q_ref, k_ref, v_ref, kc_ref, vc_ref, cos_ref, sin_ref, bias_ref, o_ref,
                 qr_s, qp_s, kr_s, v_s):
    scale = DH ** -0.5
    q = q_ref[...].astype(F32) * scale
    cos, sin = cos_ref[...], sin_ref[...]
    qp_s[...] = q.astype(BF16)
    qr_s[...] = _rope(q, cos, sin).astype(BF16)
    kr_s[...] = _rope(k_ref[...].astype(F32), cos, sin).astype(BF16)
    v_s[...] = v_ref[...].astype(BF16)
    kc = kc_ref[...].astype(BF16)
    vc = vc_ref[...].astype(BF16)
    QB, KW = ATT_RQ * GRID_W, ATT_KR * GRID_W
    nblk = GRID_H // ATT_RQ
    tb = (((1,), (1,)), ((), ()))

    def body(i, carry):
        q0 = pl.multiple_of(i * QB, QB)
        start0 = jnp.clip(ATT_RQ * i - WIN_R // 2, 0, GRID_H - ATT_KR)
        k0 = pl.multiple_of(start0 * GRID_W, GRID_W)
        var = jnp.where(i == 0, 0, jnp.where(i == nblk - 1, 2, 1))
        s_loc = lax.dot_general(qr_s[pl.ds(q0, QB), :], kr_s[pl.ds(k0, KW), :], tb,
                                preferred_element_type=F32) + bias_ref[var]
        s_ctx = lax.dot_general(qp_s[pl.ds(q0, QB), :], kc, tb, preferred_element_type=F32)
        m = jnp.maximum(jnp.max(s_loc, axis=-1, keepdims=True), jnp.max(s_ctx, axis=-1, keepdims=True))
        p_loc = jnp.exp(s_loc - m)
        p_ctx = jnp.exp(s_ctx - m)
        den = jnp.sum(p_loc, axis=-1, keepdims=True) + jnp.sum(p_ctx, axis=-1, keepdims=True)
        o = (jnp.dot(p_loc.astype(BF16), v_s[pl.ds(k0, KW), :], preferred_element_type=F32)
             + jnp.dot(p_ctx.astype(BF16), vc, preferred_element_type=F32))
        o_ref[pl.ds(q0, QB), :] = (o / den).astype(o_ref.dtype)
        return carry

    lax.fori_loop(0, nblk, body, 0)


def _attn(p, pc, c_rowblk, c_kcol, c_vcol, cos, sin, bias):
    qc, kc_, vc_ = 3 * MIX // DH, 4 * MIX // DH, 5 * MIX // DH
    lat = lambda c: pl.BlockSpec((SEQ, DH), lambda h, b: (b, c + h))
    ctx = lambda c: pl.BlockSpec((CTX, DH), lambda h, b: (c_rowblk + b, c + h))
    tab = pl.BlockSpec((SEQ, DH), lambda h, b: (0, 0))
    QB, KW = ATT_RQ * GRID_W, ATT_KR * GRID_W
    return pl.pallas_call(
        _attn_kernel,
        out_shape=jax.ShapeDtypeStruct((N_LAT, MIX), BF16),
        grid=(HEADS, BATCH),
        in_specs=[lat(qc), lat(kc_), lat(vc_), ctx(c_kcol), ctx(c_vcol), tab, tab,
                  pl.BlockSpec((None, 3, QB, KW), lambda h, b: (h, 0, 0, 0))],
        out_specs=pl.BlockSpec((SEQ, DH), lambda h, b: (b, h)),
        scratch_shapes=[pltpu.VMEM((SEQ, DH), BF16)] * 4,
        compiler_params=_cp(("arbitrary", "arbitrary"), 40),
    )(p, p, p, pc, pc, cos, sin, bias)


def _ctxattn_kernel(q_ref, k_ref, v_ref, o_ref):
    scale = DH ** -0.5
    q = (q_ref[...].astype(F32) * scale).astype(BF16)
    s = lax.dot_general(q, k_ref[...].astype(BF16), (((1,), (1,)), ((), ())), preferred_element_type=F32)
    m = jnp.max(s, axis=-1, keepdims=True)
    e = jnp.exp(s - m)
    den = jnp.sum(e, axis=-1, keepdims=True)
    o = jnp.dot(e.astype(BF16), v_ref[...].astype(BF16), preferred_element_type=F32)
    o_ref[...] = (o / den).astype(o_ref.dtype)


def _ctxattn(p):
    rb = N_LAT // CTX
    qc, kc_, vc_ = 3 * MIX // DH, 4 * MIX // DH, 5 * MIX // DH
    blk = lambda c: pl.BlockSpec((CTX, DH), lambda b, h: (rb + b, c + h))
    return pl.pallas_call(
        _ctxattn_kernel,
        out_shape=jax.ShapeDtypeStruct((N_CTX, MIX), BF16),
        grid=(BATCH, HEADS),
        in_specs=[blk(qc), blk(kc_), blk(vc_)],
        out_specs=pl.BlockSpec((CTX, DH), lambda b, h: (b, h)),
        compiler_params=_cp(("arbitrary", "arbitrary"), 32),
    )(p, p, p)


def _pad2(a, r, c):
    return jnp.pad(a, ((0, r - a.shape[0]), (0, c - a.shape[1])))


def kernel(x, c, ctx, c_ctx, ada_w, ada_b, norm_g, ffn1_in, ffn1_out, ffn2_in, ffn2_out, w_in, pool_w, pool_scale, conv_dw_w, conv_dw_b, conv_ln_g, conv_ln_b, na_rpb, hy_short_w, hy_short_b, hy_w1, hy_b1, hy_freq1, hy_w2, hy_b2, hy_freq2, hy_w3, hy_decay, hy_bias, w_branch, w_out):
    X = jnp.concatenate([x.reshape(N_LAT, D), ctx.reshape(N_CTX, D)], axis=0)
    cc = jnp.concatenate([c, c_ctx[None], jnp.zeros((8 - BATCH - 1, D), F32)], axis=0)
    g3 = norm_g.reshape(DEPTH * 6, 1, D)
    ada_b3 = ada_b.reshape(DEPTH, 1, N_ADA * D)
    vec3 = lambda a: a.reshape(DEPTH, 1, a.shape[-1])
    pool_scale3, conv_dw_b3, conv_ln_g3, conv_ln_b3 = map(vec3, (pool_scale, conv_dw_b, conv_ln_g, conv_ln_b))
    hy_short_b3, hy_decay3 = vec3(hy_short_b), vec3(hy_decay)
    hy_bias4 = hy_bias.reshape(DEPTH, 2, 1, MIX)

    cos, sin = _rope_tables()
    bands = {SEQ: _pool_bands(SEQ), CTX: _pool_bands(CTX)}
    dft = {SEQ: _dft_mats(SEQ), CTX: _dft_mats(CTX)}
    feats = {SEQ: _hy_feats(SEQ), CTX: _hy_feats(CTX)}

    m3 = [_ada(cc, ada_w, ada_b3, l).reshape(8, 1, N_ADA * D) for l in range(DEPTH)]

    rows = N_ALL
    h = _normmod(X, g3, m3[0], 0, 0, 0, rows)
    for l in range(DEPTH):
        last = l == DEPTH - 1
        m = m3[l]

        hid = _ffn_up(h, ffn1_in, l, rows)
        y = _matmul(hid, ffn1_out, l, rows, D, F32, bm=512, bn=512)
        X, h = _resid(X, y, g3, m, l, 1, 2, 0.5, rows, nxt=(g3, m, l, 2, 3))

        if not last:
            p = _matmul(h, w_in, l, rows, PROJ_W, P_DT, bm=1024, bn=512)
            pc, c_rowblk, c_kcol, c_vcol = p, N_LAT // CTX, 4 * MIX // DH, 5 * MIX // DH
        else:
            p = _matmul(h, w_in, l, N_LAT, PROJ_W, P_DT, bm=1024, bn=512)
            pc = _matmul(h, w_in, l, N_CTX, 2 * MIX, P_DT, bm=1024, bn=512, row_off=N_LAT, col_off=4 * MIX)
            c_rowblk, c_kcol, c_vcol = 0, 0, MIX // DH

        fpar = (_pad2(hy_w1[l], 128, 128), _pad2(hy_b1[l][None], 1, 128), _pad2(hy_freq1[l][None], 1, 128),
                _pad2(hy_w2[l], 128, 128), _pad2(hy_b2[l][None], 1, 128), _pad2(hy_freq2[l][None], 1, 128),
                _pad2(hy_w3[l], 128, 4 * MIX))

        def mixers(S, nseq, row_off):
            F, G = dft[S]
            KF = _kf(F, _hyfilt(feats[S], *fpar, hy_decay3, l, S), S)
            y0 = _pool(p, bands[S], pool_w, pool_scale3, l, S, nseq, row_off)
            y1 = _conv(p, conv_dw_w, conv_dw_b3, conv_ln_g3, conv_ln_b3, l, S, nseq, row_off)
            y3 = _hyena(p, F, G, KF, hy_short_w, hy_short_b3, hy_bias4, l, S, nseq, row_off)
            return y0, y1, y3

        y0, y1, y3 = mixers(SEQ, BATCH, 0)
        y2 = _attn(p, pc, c_rowblk, c_kcol, c_vcol, cos, sin, _attn_bias(na_rpb[l]))
        if not last:
            c0, c1, c3 = mixers(CTX, BATCH, N_LAT)
            c2 = _ctxattn(p)
            ys = [jnp.concatenate(pair, axis=0) for pair in ((y0, c0), (y1, c1), (y2, c2), (y3, c3))]
            mrows = rows
        else:
            ys = [y0, y1, y2, y3]
            mrows = N_LAT
        merged = _branch(ys, p, w_branch, l, mrows)
        y = _matmul(merged, w_out, l, mrows, D, F32, bm=1024, bn=512)
        X, h = _resid(X, y, g3, m, l, 3, 5, 1.0, mrows, nxt=(g3, m, l, 4, 6))
        rows = mrows

        hid = _ffn_up(h, ffn2_in, l, rows)
        y = _matmul(hid, ffn2_out, l, rows, D, F32, bm=512, bn=512)
        if not last:
            X, h = _resid(X, y, g3, m, l, 5, 8, 0.5, rows, nxt=(g3, m3[l + 1], l + 1, 0, 0))
        else:
            X = _resid(X, y, g3, m, l, 5, 8, 0.5, rows)
    return X.reshape(BATCH, SEQ, D)
```

```python
import functools
import math

import numpy as np
import jax
import jax.numpy as jnp
from jax import lax
from jax.experimental import pallas as pl
from jax.experimental.pallas import tpu as pltpu

F32 = jnp.float32
BF16 = jnp.bfloat16

D = 4096
BATCH = 4
SEQ = 2048
DEPTH = 2
GRID_W = 64
GRID_H = SEQ // GRID_W
CTX = 256
D_FF = 6144
MIX = D // 4
N_BRANCH = 4
POOL_WINDOWS = (2, 4, 8, 16)
POOL_G = MIX // len(POOL_WINDOWS)
CONV_K = 31
DH = 128
HEADS = MIX // DH
WIN_R = 8
WIN_C = 16
ROPE_BASE = 10000.0
HY_EMB = 33
HY_HID = 64
N_ADA = 9
EPS = 1e-6
PROJ_W = 9 * MIX + N_BRANCH * D
G_OFF = 9 * MIX

N_LAT = BATCH * SEQ
N_CTX = BATCH * CTX
N_ALL = N_LAT + N_CTX

P_DT = BF16
NEG = -1e30

ATT_RQ = 4
ATT_KR = 12
MIB = 1024 * 1024
STREAM_VMEM_MIB = 60


def _cp(sem, vmem_mib):
    return pltpu.CompilerParams(dimension_semantics=sem, vmem_limit_bytes=vmem_mib * MIB)


def _modrow(i, bm):
    return jnp.minimum((i * bm) // SEQ, BATCH)


def _silu(v):
    return v * jax.nn.sigmoid(v)


def _ada_kernel(c_ref, w_ref, b_ref, o_ref):
    s = _silu(c_ref[...]).astype(BF16)
    o_ref[...] = jnp.dot(s, w_ref[...].astype(BF16), preferred_element_type=F32) + b_ref[...]


def _ada(cc, ada_w, ada_b3, l):
    bn = 512
    return pl.pallas_call(
        _ada_kernel, name="ada",
        out_shape=jax.ShapeDtypeStruct((8, N_ADA * D), F32),
        grid=(N_ADA * D // bn,),
        in_specs=[pl.BlockSpec((8, D), lambda j: (0, 0)),
                  pl.BlockSpec((None, D, bn), lambda j: (l, 0, j)),
                  pl.BlockSpec((None, 1, bn), lambda j: (l, 0, j))],
        out_specs=pl.BlockSpec((8, bn), lambda j: (0, j)),
        compiler_params=_cp(("arbitrary",), 40),
    )(cc, ada_w, ada_b3)


def _rms(v, g):
    return v * lax.rsqrt(jnp.mean(v * v, axis=-1, keepdims=True) + EPS) * g


def _normmod_kernel(x_ref, g_ref, sh_ref, sc_ref, h_ref):
    y = _rms(x_ref[...], g_ref[...])
    h_ref[...] = (y * (1.0 + sc_ref[...]) + sh_ref[...]).astype(h_ref.dtype)


def _mod_spec(bm, k):
    return pl.BlockSpec((None, 1, D), lambda i: (_modrow(i, bm), 0, k))


def _g_spec(l, k):
    return pl.BlockSpec((None, 1, D), lambda i: (l * 6 + k, 0, 0))


def _normmod(x, g3, m3, l, gk, mk, rows):
    bm = 256
    return pl.pallas_call(
        _normmod_kernel, name="normmod",
        out_shape=jax.ShapeDtypeStruct((rows, D), BF16),
        grid=(rows // bm,),
        in_specs=[pl.BlockSpec((bm, D), lambda i: (i, 0)), _g_spec(l, gk),
                  _mod_spec(bm, mk), _mod_spec(bm, mk + 1)],
        out_specs=pl.BlockSpec((bm, D), lambda i: (i, 0)),
        compiler_params=_cp(("arbitrary",), 32),
    )(x, g3, m3, m3)


def _resid_kernel(x_ref, y_ref, gp_ref, gate_ref, *rest, coef, with_next):
    xn = x_ref[...] + coef * gate_ref[...] * _rms(y_ref[...].astype(F32), gp_ref[...])
    if with_next:
        gn_ref, sh_ref, sc_ref, xo_ref, h_ref = rest
        xo_ref[...] = xn
        h_ref[...] = (_rms(xn, gn_ref[...]) * (1.0 + sc_ref[...]) + sh_ref[...]).astype(h_ref.dtype)
    else:
        (xo_ref,) = rest
        xo_ref[...] = xn


def _resid(x, y, g3, m3, l, gk_post, mk_gate, coef, rows, nxt=None):
    bm = 256
    in_specs = [pl.BlockSpec((bm, D), lambda i: (i, 0)), pl.BlockSpec((bm, D), lambda i: (i, 0)),
                _g_spec(l, gk_post), _mod_spec(bm, mk_gate)]
    args = [x, y, g3, m3]
    out_shape = [jax.ShapeDtypeStruct((rows, D), F32)]
    out_specs = [pl.BlockSpec((bm, D), lambda i: (i, 0))]
    if nxt is not None:
        g3n, m3n, ln, gkn, mkn = nxt
        in_specs += [_g_spec(ln, gkn), _mod_spec(bm, mkn), _mod_spec(bm, mkn + 1)]
        args += [g3n, m3n, m3n]
        out_shape.append(jax.ShapeDtypeStruct((rows, D), BF16))
        out_specs.append(pl.BlockSpec((bm, D), lambda i: (i, 0)))
    res = pl.pallas_call(
        functools.partial(_resid_kernel, coef=coef, with_next=nxt is not None), name="resid",
        out_shape=out_shape, grid=(rows // bm,), in_specs=in_specs, out_specs=out_specs,
        compiler_params=_cp(("arbitrary",), 48),
    )(*args)
    return res if nxt is not None else res[0]


def _stream_kernel(*refs, n_x, n_w, n_e, combine):
    xs = refs[:n_x]
    ws = refs[n_x:n_x + n_w]
    es = refs[n_x + n_w:n_x + n_w + n_e]
    o_ref = refs[n_x + n_w + n_e]
    slots = refs[n_x + n_w + n_e + 1:]
    slot_a, slot_b = slots[:n_w], slots[n_w:]
    j, i = pl.program_id(0), pl.program_id(1)

    def stage(dst):
        for w_ref, d in zip(ws, dst):
            ck = w_ref.shape[-2]
            r0 = pl.multiple_of(i * ck, ck)
            if len(w_ref.shape) == 3:
                d[:, pl.ds(r0, ck), :] = w_ref[...].astype(BF16)
            else:
                d[pl.ds(r0, ck), :] = w_ref[...].astype(BF16)

    def compute(src):
        o_ref[...] = combine(xs, src, es).astype(o_ref.dtype)

    @pl.when(j == 0)
    def _():
        stage(slot_a)
        o_ref[...] = jnp.zeros(o_ref.shape, o_ref.dtype)

    @pl.when(j % 2 == 1)
    def _():
        stage(slot_b)
        compute(slot_a)

    @pl.when((j > 0) & (j % 2 == 0))
    def _():
        stage(slot_a)
        compute(slot_b)


def _comb_plain(xs, ws, es):
    return jnp.dot(xs[0][...], ws[0][...], preferred_element_type=F32)


def _comb_swiglu(xs, ws, es):
    x = xs[0][...]
    a = jnp.dot(x, ws[0][...], preferred_element_type=F32)
    b = jnp.dot(x, ws[1][...], preferred_element_type=F32)
    return _silu(a) * b


def _comb_branch(xs, ws, es):
    acc = None
    for br in range(N_BRANCH):
        t = jax.nn.sigmoid(es[br][...].astype(F32)) * jnp.dot(xs[br][...], ws[0][br], preferred_element_type=F32)
        acc = t if acc is None else acc + t
    return acc


def _row_idx(j, i):
    return jnp.where(j == 0, 0, i)


def _stream_call(name, combine, xs, x_k, ws, w_specs, w_slots, es, e_specs, rows, n_out, out_dtype,
                 bm, bn, ro, vmem):
    nj, ni = n_out // bn, rows // bm
    x_specs = [pl.BlockSpec((bm, x_k), lambda j, i: (_row_idx(j, i) + ro, 0))] * len(xs)
    return pl.pallas_call(
        functools.partial(_stream_kernel, n_x=len(xs), n_w=len(ws), n_e=len(es), combine=combine), name=name,
        out_shape=jax.ShapeDtypeStruct((rows, n_out), out_dtype),
        grid=(nj + 1, ni),
        in_specs=x_specs + w_specs + e_specs,
        out_specs=pl.BlockSpec((bm, bn), lambda j, i: (_row_idx(j, i), jnp.maximum(j - 1, 0))),
        scratch_shapes=w_slots + w_slots,
        compiler_params=_cp(("arbitrary", "arbitrary"), vmem),
    )(*xs, *ws, *es)


def _matmul(x, w, l, rows, n_out, out_dtype, *, name, ni, bn, row_off=0, col_off=0, vmem=STREAM_VMEM_MIB):
    K = x.shape[1]
    bm, ck, nj = rows // ni, K // ni, n_out // bn
    co = col_off // bn
    w_specs = [pl.BlockSpec((None, ck, bn), lambda j, i: (l, i, jnp.minimum(j, nj - 1) + co))]
    return _stream_call(name, _comb_plain, [x], K, [w], w_specs, [pltpu.VMEM((K, bn), BF16)], [], [],
                        rows, n_out, out_dtype, bm, bn, row_off // bm, vmem)


def _ffn_up(h, w_up, l, rows):
    ni, bn = 8, 512
    bm, ck, nj = rows // ni, D // ni, D_FF // bn
    w_specs = [pl.BlockSpec((None, ck, bn), lambda j, i: (l, i, jnp.minimum(j, nj - 1))),
               pl.BlockSpec((None, ck, bn), lambda j, i: (l, i, jnp.minimum(j, nj - 1) + nj))]
    return _stream_call("ffn_up", _comb_swiglu, [h], D, [w_up, w_up], w_specs, [pltpu.VMEM((D, bn), BF16)] * 2,
                        [], [], rows, D_FF, BF16, bm, bn, 0, STREAM_VMEM_MIB)


def _branch(ys, p, w_branch, l, rows):
    ni, bn = 8, 512
    bm, ck, nj = rows // ni, MIX // ni, D // bn
    w_specs = [pl.BlockSpec((None, N_BRANCH, ck, bn), lambda j, i: (l, 0, i, jnp.minimum(j, nj - 1)))]
    e_specs = [pl.BlockSpec((bm, bn), functools.partial(
        lambda j, i, br: (_row_idx(j, i), (G_OFF + br * D) // bn + jnp.maximum(j - 1, 0)), br=br))
        for br in range(N_BRANCH)]
    return _stream_call("branch_merge", _comb_branch, list(ys), MIX, [w_branch], w_specs,
                        [pltpu.VMEM((N_BRANCH, MIX, bn), BF16)], [p] * N_BRANCH, e_specs,
                        rows, D, BF16, bm, bn, 0, STREAM_VMEM_MIB)


def _pool_bands(S):
    t = jnp.arange(S, dtype=jnp.int32)[:, None]
    m = jnp.arange(S, dtype=jnp.int32)[None, :]
    bands = []
    for w in POOL_WINDOWS:
        lo = jnp.clip(t - w // 2, 0, S)
        hi = jnp.clip(t + w // 2, 0, S)
        inside = (m >= lo) & (m < hi)
        band = jnp.where(inside, 1.0 / (hi - lo).astype(F32), 0.0) - (m == t).astype(F32)
        bands.append(band)
    return jnp.stack(bands).astype(BF16)


def _pool_kernel(band_ref, p_ref, w_ref, sc_ref, o_ref):
    y = jnp.dot(band_ref[...], p_ref[...].astype(BF16), preferred_element_type=F32)
    z = jnp.dot(y.astype(BF16), w_ref[...].astype(BF16), preferred_element_type=F32)
    o_ref[...] = (z * sc_ref[...]).astype(o_ref.dtype)


def _pool(p, bands, pool_w, pool_scale3, l, S, nseq, row_off):
    ro = row_off // S
    ng = len(POOL_WINDOWS)
    return pl.pallas_call(
        _pool_kernel, name="pool",
        out_shape=jax.ShapeDtypeStruct((nseq * S, MIX), BF16),
        grid=(ng, nseq),
        in_specs=[pl.BlockSpec((None, S, S), lambda g, s: (g, 0, 0)),
                  pl.BlockSpec((S, POOL_G), lambda g, s: (s + ro, g)),
                  pl.BlockSpec((None, None, POOL_G, POOL_G), lambda g, s: (l, g, 0, 0)),
                  pl.BlockSpec((None, 1, POOL_G), lambda g, s: (l, 0, g))],
        out_specs=pl.BlockSpec((S, POOL_G), lambda g, s: (s, g)),
        compiler_params=_cp(("arbitrary", "arbitrary"), 40),
    )(bands, p, pool_w, pool_scale3)


CONV_PAD = 16
CONV_CH = 32


def _conv_kernel(a_ref, g_ref, w_ref, b_ref, lg_ref, lb_ref, o_ref, u_ref, *, S):
    C = a_ref.shape[1]
    zeros = jnp.zeros((CONV_PAD, C), F32)
    u_ref[0:CONV_PAD, :] = zeros
    u_ref[S + CONV_PAD:S + 2 * CONV_PAD, :] = zeros
    SC = 128

    def stage(i, carry):
        r0 = pl.multiple_of(i * SC, SC)
        a = a_ref[pl.ds(r0, SC), :].astype(F32)
        g = g_ref[pl.ds(r0, SC), :].astype(F32)
        u_ref[pl.ds(r0 + CONV_PAD, SC), :] = a * jax.nn.sigmoid(g)
        return carry

    lax.fori_loop(0, S // SC, stage, 0)

    n = CONV_CH + 2 * CONV_PAD
    off = CONV_PAD - CONV_K // 2

    def body(i, carry):
        r0 = pl.multiple_of(i * CONV_CH, CONV_CH)
        win = u_ref[pl.ds(r0, n), :]
        acc = jnp.broadcast_to(b_ref[...], (CONV_CH, C))
        for sub in range(8):
            wsub = win if sub == 0 else pltpu.roll(win, n - sub, axis=0)
            for al in range(n // 8):
                k = 8 * al + sub - off
                if 0 <= k < CONV_K:
                    acc = acc + w_ref[k:k + 1, :] * wsub[8 * al:8 * al + CONV_CH]
        mu = jnp.mean(acc, axis=-1, keepdims=True)
        xc = acc - mu
        var = jnp.mean(xc * xc, axis=-1, keepdims=True)
        y = xc * lax.rsqrt(var + EPS) * lg_ref[...] + lb_ref[...]
        o_ref[pl.ds(r0, CONV_CH), :] = _silu(y).astype(o_ref.dtype)
        return carry

    lax.fori_loop(0, S // CONV_CH, body, 0)


def _conv(p, dw_w, dw_b3, ln_g3, ln_b3, l, S, nseq, row_off):
    ro = row_off // S
    vec = lambda: pl.BlockSpec((None, 1, MIX), lambda s: (l, 0, 0))
    return pl.pallas_call(
        functools.partial(_conv_kernel, S=S), name="conv_module",
        out_shape=jax.ShapeDtypeStruct((nseq * S, MIX), BF16),
        grid=(nseq,),
        in_specs=[pl.BlockSpec((S, MIX), lambda s: (s + ro, 1)),
                  pl.BlockSpec((S, MIX), lambda s: (s + ro, 2)),
                  pl.BlockSpec((None, CONV_K, MIX), lambda s: (l, 0, 0)),
                  vec(), vec(), vec()],
        out_specs=pl.BlockSpec((S, MIX), lambda s: (s, 0)),
        scratch_shapes=[pltpu.VMEM((S + 2 * CONV_PAD, MIX), F32)],
        compiler_params=_cp(("arbitrary",), 48),
    )(p, p, dw_w, dw_b3, ln_g3, ln_b3)


HY_PAD = 8
HY_CH = 64


def _hyshort_kernel(p_ref, w_ref, b_ref, o_ref, u_ref, *, S):
    C = p_ref.shape[1]
    zeros = jnp.zeros((HY_PAD, C), F32)
    u_ref[0:HY_PAD, :] = zeros
    u_ref[S + HY_PAD:S + 2 * HY_PAD, :] = zeros
    SC = 128

    def stage(i, carry):
        r0 = pl.multiple_of(i * SC, SC)
        u_ref[pl.ds(r0 + HY_PAD, SC), :] = p_ref[pl.ds(r0, SC), :].astype(F32)
        return carry

    lax.fori_loop(0, S // SC, stage, 0)
    n = HY_CH + 2 * HY_PAD

    def body(i, carry):
        r0 = pl.multiple_of(i * HY_CH, HY_CH)
        win = u_ref[pl.ds(r0, n), :]
        prev = pltpu.roll(win, n - (HY_PAD - 1), axis=0)[0:HY_CH]
        mid = win[HY_PAD:HY_PAD + HY_CH]
        nxt = pltpu.roll(win, n - 1, axis=0)[HY_PAD:HY_PAD + HY_CH]
        o_ref[pl.ds(r0, HY_CH), :] = (w_ref[0:1, :] * prev + w_ref[1:2, :] * mid + w_ref[2:3, :] * nxt
                                       + b_ref[...])
        return carry

    lax.fori_loop(0, S // HY_CH, body, 0)


def _hyshort(p, sw, sb3, l, S, nseq, row_off):
    ro = row_off // S
    cb = (6 * MIX) // MIX
    return pl.pallas_call(
        functools.partial(_hyshort_kernel, S=S), name="hy_short",
        out_shape=jax.ShapeDtypeStruct((nseq * S, 3 * MIX), F32),
        grid=(nseq, 3),
        in_specs=[pl.BlockSpec((S, MIX), lambda s, c: (s + ro, cb + c)),
                  pl.BlockSpec((None, 3, MIX), lambda s, c: (l, 0, c)),
                  pl.BlockSpec((None, 1, MIX), lambda s, c: (l, 0, c))],
        out_specs=pl.BlockSpec((S, MIX), lambda s, c: (s, c)),
        scratch_shapes=[pltpu.VMEM((S + 2 * HY_PAD, MIX), F32)],
        compiler_params=_cp(("arbitrary", "arbitrary"), 48),
    )(p, sw, sb3)


def _hy_feats(L):
    t = jnp.linspace(0.0, 1.0, L, dtype=F32)[:, None]
    omega = (2.0 * math.pi / L) * jnp.arange(L, dtype=F32)[:, None]
    bands = (HY_EMB - 1) // 2
    freqs = jnp.linspace(1e-4, bands - 1, bands, dtype=F32)[None, :]
    z = jnp.concatenate([t, jnp.cos(freqs * omega), -jnp.sin(freqs * omega)], axis=-1)
    return jnp.pad(z, ((0, 0), (0, 128 - HY_EMB)))


def _hyfilt_kernel(z_ref, w1_ref, b1_ref, f1_ref, w2_ref, b2_ref, f2_ref, w3_ref, dec_ref, o_ref, *, L):
    CH = 256 if L >= 256 else L
    hp = lax.Precision.HIGHEST
    C = o_ref.shape[1]

    def body(i, carry):
        r0 = pl.multiple_of(i * CH, CH)
        z = z_ref[pl.ds(r0, CH), :]
        h = jnp.sin(f1_ref[...] * (jnp.dot(z, w1_ref[...], precision=hp, preferred_element_type=F32) + b1_ref[...]))
        h = jnp.sin(f2_ref[...] * (jnp.dot(h, w2_ref[...], precision=hp, preferred_element_type=F32) + b2_ref[...]))
        o = jnp.dot(h, w3_ref[...], precision=hp, preferred_element_type=F32)
        t = (lax.broadcasted_iota(jnp.int32, (CH, C), 0) + r0).astype(F32) * (1.0 / (L - 1))
        o_ref[pl.ds(r0, CH), :] = o * jnp.exp(-t * jnp.abs(dec_ref[...]))
        return carry

    lax.fori_loop(0, L // CH, body, 0)


def _hyfilt(zf, w1p, b1p, f1p, w2p, b2p, f2p, w3p, dec3, l, L):
    bn = 1024
    nfc = 4 * MIX
    small = lambda shp: pl.BlockSpec(shp, lambda j: (0,) * len(shp))
    return pl.pallas_call(
        functools.partial(_hyfilt_kernel, L=L), name="hy_filter",
        out_shape=jax.ShapeDtypeStruct((L, nfc), F32),
        grid=(nfc // bn,),
        in_specs=[small((L, 128)), small((128, 128)), small((1, 128)), small((1, 128)),
                  small((128, 128)), small((1, 128)), small((1, 128)),
                  pl.BlockSpec((128, bn), lambda j: (0, j)),
                  pl.BlockSpec((None, 1, bn), lambda j: (l, 0, j))],
        out_specs=pl.BlockSpec((L, bn), lambda j: (0, j)),
        compiler_params=_cp(("arbitrary",), 40),
    )(zf, w1p, b1p, f1p, w2p, b2p, f2p, w3p, dec3)


def _dft_mats(L):
    N = 2 * L
    a = jnp.arange(L, dtype=jnp.int32)
    idx = (a[:, None] * a[None, :]) & (N - 1)
    ang = idx.astype(F32) * (2.0 * math.pi / N)
    cs, sn = jnp.cos(ang), jnp.sin(ang)
    alt = (1 - 2 * (a & 1)).astype(F32)
    first = (a == 0)
    fs = jnp.where(first[:, None], alt[None, :], -sn)
    F = jnp.stack([cs, fs]).astype(BF16)
    wf = jnp.where(first, 1.0 / N, 2.0 / N).astype(BF16)
    G = jnp.concatenate([F[0] * wf[None, :], F[1].T * wf[None, :]], axis=1)
    return F, G


def _kf_kernel(fc_ref, fs_ref, hf_ref, hb_ref, o_ref, hs_s, hf_s, hb_s):
    @pl.when(pl.program_id(2) == 0)
    def _():
        hf = hf_ref[...]
        rows = lax.broadcasted_iota(jnp.int32, hf.shape, 0)
        hb = jnp.where(rows == 0, 0.0, hb_ref[...])
        hs_s[...] = (hf + hb).astype(BF16)
        hf_s[...] = hf.astype(BF16)
        hb_s[...] = hb.astype(BF16)

    kr = jnp.dot(fc_ref[...], hs_s[...], preferred_element_type=F32)
    ia = jnp.dot(fs_ref[...], hf_s[...], preferred_element_type=F32)
    ib = jnp.dot(fs_ref[...], hb_s[...], preferred_element_type=F32)
    bf = kr.shape[0]
    frow = lax.broadcasted_iota(jnp.int32, kr.shape, 0) + pl.program_id(2) * bf
    o_ref[0] = kr
    o_ref[1] = jnp.where(frow == 0, ia + ib, ia - ib)


def _kf(F, hfilt, L):
    bf = min(512, L)
    cw = 512
    ncb = MIX // cw
    return pl.pallas_call(
        _kf_kernel, name="hy_filter_dft",
        out_shape=jax.ShapeDtypeStruct((2, 2, L, MIX), F32),
        grid=(2, ncb, L // bf),
        in_specs=[pl.BlockSpec((None, bf, L), lambda o, c, f: (0, f, 0)),
                  pl.BlockSpec((None, bf, L), lambda o, c, f: (1, f, 0)),
                  pl.BlockSpec((L, cw), lambda o, c, f: (0, o * ncb + c)),
                  pl.BlockSpec((L, cw), lambda o, c, f: (0, (2 + o) * ncb + c))],
        out_specs=pl.BlockSpec((None, 2, bf, cw), lambda o, c, f: (o, 0, f, c)),
        scratch_shapes=[pltpu.VMEM((L, cw), BF16)] * 3,
        compiler_params=_cp(("arbitrary", "arbitrary", "arbitrary"), 48),
    )(F, F, hfilt, hfilt)


def _hyfwd_kernel(z_ref, fc_ref, fs_ref, kr_ref, ki_ref, o_ref, zbf_ref):
    @pl.when(pl.program_id(1) == 0)
    def _():
        zbf_ref[...] = z_ref[...].astype(BF16)

    zb = zbf_ref[...]
    zr = jnp.dot(fc_ref[...], zb, preferred_element_type=F32)
    zi = jnp.dot(fs_ref[...], zb, preferred_element_type=F32)
    kr, ki = kr_ref[...], ki_ref[...]
    bf = zr.shape[0]
    frow = lax.broadcasted_iota(jnp.int32, zr.shape, 0) + pl.program_id(1) * bf
    dc = frow == 0
    o_ref[0] = jnp.where(dc, zr * kr, zr * kr - zi * ki).astype(o_ref.dtype)
    o_ref[1] = jnp.where(dc, zi * ki, zr * ki + zi * kr).astype(o_ref.dtype)


def _hyfwd(z, zcol, F, KF, o, S, nseq):
    bf = min(512, S)
    return pl.pallas_call(
        _hyfwd_kernel, name="hy_fwd_dft",
        out_shape=jax.ShapeDtypeStruct((nseq, 2, S, MIX), BF16),
        grid=(nseq, S // bf),
        in_specs=[pl.BlockSpec((S, MIX), lambda s, f: (s, zcol)),
                  pl.BlockSpec((None, bf, S), lambda s, f: (0, f, 0)),
                  pl.BlockSpec((None, bf, S), lambda s, f: (1, f, 0)),
                  pl.BlockSpec((None, None, bf, MIX), lambda s, f: (o, 0, f, 0)),
                  pl.BlockSpec((None, None, bf, MIX), lambda s, f: (o, 1, f, 0))],
        out_specs=pl.BlockSpec((None, 2, bf, MIX), lambda s, f: (s, 0, f, 0)),
        scratch_shapes=[pltpu.VMEM((S, MIX), BF16)],
        compiler_params=_cp(("arbitrary", "arbitrary"), 48),
    )(z, F, F, KF, KF)


def _hyinv_kernel(g_ref, p_ref, z_ref, gate_ref, skip_ref, o_ref):
    y = jnp.dot(g_ref[...], p_ref[...], preferred_element_type=F32)
    y = y + z_ref[...] * skip_ref[...]
    o_ref[...] = (gate_ref[...] * y).astype(o_ref.dtype)


def _hyinv(G, P, z, zcol, u3, gcol, hy_bias, l, o, S, nseq, out_dtype):
    bn = min(512, S)
    nb = S // bn
    P2 = P.reshape(nseq, 2 * S, MIX)
    return pl.pallas_call(
        _hyinv_kernel, name="hy_inv_dft",
        out_shape=jax.ShapeDtypeStruct((nseq * S, MIX), out_dtype),
        grid=(nseq, nb),
        in_specs=[pl.BlockSpec((bn, 2 * S), lambda s, n: (n, 0)),
                  pl.BlockSpec((None, 2 * S, MIX), lambda s, n: (s, 0, 0)),
                  pl.BlockSpec((bn, MIX), lambda s, n: (s * nb + n, zcol)),
                  pl.BlockSpec((bn, MIX), lambda s, n: (s * nb + n, gcol)),
                  pl.BlockSpec((None, None, 1, MIX), lambda s, n: (l, o, 0, 0))],
        out_specs=pl.BlockSpec((bn, MIX), lambda s, n: (s * nb + n, 0)),
        compiler_params=_cp(("arbitrary", "arbitrary"), 48),
    )(G, P2, z, u3, hy_bias)


def _hyena(p, F, G, KF, hy_short_w, hy_short_b3, hy_bias4, l, S, nseq, row_off):
    u3 = _hyshort(p, hy_short_w, hy_short_b3, l, S, nseq, row_off)
    P = _hyfwd(u3, 0, F, KF, 0, S, nseq)
    z1 = _hyinv(G, P, u3, 0, u3, 1, hy_bias4, l, 0, S, nseq, F32)
    P = _hyfwd(z1, 0, F, KF, 1, S, nseq)
    return _hyinv(G, P, z1, 0, u3, 2, hy_bias4, l, 1, S, nseq, BF16)


def _rope_tables():
    d_axis = DH // 2
    inv = ROPE_BASE ** (-jnp.arange(0, d_axis, 2, dtype=F32) / d_axis)
    t = jnp.arange(SEQ)
    pos = jnp.stack([t // GRID_W, t % GRID_W], axis=-1).astype(F32)
    lane = np.arange(DH)
    ang = pos[:, lane // d_axis] * inv[lane % (d_axis // 2)][None, :]
    sign = np.where((lane % d_axis) < d_axis // 2, -1.0, 1.0).astype(np.float32)
    return jnp.cos(ang), jnp.sin(ang) * sign[None, :]


def _toeplitz_kernel(r_ref, e_ref, o_ref):
    o_ref[...] = jnp.dot(r_ref[...], e_ref[...], precision=lax.Precision.HIGHEST, preferred_element_type=F32)


def _attn_bias(rpb_l):
    n_dr, n_dc = 2 * WIN_R - 1, 2 * WIN_C - 1
    d = lax.broadcasted_iota(jnp.int32, (128, GRID_W * GRID_W), 0)
    cw = lax.broadcasted_iota(jnp.int32, (128, GRID_W * GRID_W), 1)
    c, w = cw // GRID_W, cw % GRID_W
    c0 = jnp.clip(c - WIN_C // 2, 0, GRID_W - WIN_C)
    col_in = (w >= c0) & (w < c0 + WIN_C)
    sel = jnp.where(col_in, d == jnp.clip(w - c + (WIN_C - 1), 0, n_dc - 1), d == n_dc).astype(F32)
    tab = jnp.concatenate([rpb_l.reshape(HEADS * n_dr, n_dc).astype(F32),
                           jnp.full((HEADS * n_dr, 1), NEG, F32)], axis=1)
    tab = jnp.pad(tab, ((0, 128 - HEADS * n_dr), (0, 128 - n_dc - 1)))
    toe = pl.pallas_call(
        _toeplitz_kernel, name="na_bias_toeplitz",
        out_shape=jax.ShapeDtypeStruct((128, GRID_W * GRID_W), F32),
    )(tab, sel)
    toe = toe[:HEADS * n_dr].reshape(HEADS, n_dr, GRID_W, GRID_W)
    masked = jnp.full((HEADS, GRID_W, GRID_W), NEG, F32)
    nblk = GRID_H // ATT_RQ
    out = []
    for blk in (0, 1, nblk - 1):
        start0 = int(np.clip(ATT_RQ * blk - WIN_R // 2, 0, GRID_H - ATT_KR))
        rows = []
        for j in range(ATT_RQ):
            r = ATT_RQ * blk + j
            rs = int(np.clip(r - WIN_R // 2, 0, GRID_H - WIN_R))
            tiles = []
            for ir in range(ATT_KR):
                kr = start0 + ir
                tiles.append(toe[:, kr - r + (WIN_R - 1)] if rs <= kr < rs + WIN_R else masked)
            rows.append(jnp.concatenate(tiles, axis=-1))
        out.append(jnp.concatenate(rows, axis=1))
    return jnp.stack(out, axis=1)


def _rope(x, cos, sin_signed):
    lane = lax.broadcasted_iota(jnp.int32, x.shape, 1)
    partner = jnp.where((lane & 32) == 0, pltpu.roll(x, DH - 32, axis=1), pltpu.roll(x, 32, axis=1))
    return x * cos + partner * sin_signed


def _attn_kernel(q_ref, k_ref, v_ref, kc_ref, vc_ref, cos_ref, sin_ref, bias_ref, o_ref,
                 qr_s, qp_s, kr_s, v_s):
    scale = DH ** -0.5
    q = q_ref[...].astype(F32) * scale
    cos, sin = cos_ref[...], sin_ref[...]
    qp_s[...] = q.astype(BF16)
    qr_s[...] = _rope(q, cos, sin).astype(BF16)
    kr_s[...] = _rope(k_ref[...].astype(F32), cos, sin).astype(BF16)
    v_s[...] = v_ref[...].astype(BF16)
    kc = kc_ref[...].astype(BF16)
    vc = vc_ref[...].astype(BF16)
    QB, KW = ATT_RQ * GRID_W, ATT_KR * GRID_W
    nblk = GRID_H // ATT_RQ
    tb = (((1,), (1,)), ((), ()))

    def body(i, carry):
        q0 = pl.multiple_of(i * QB, QB)
        start0 = jnp.clip(ATT_RQ * i - WIN_R // 2, 0, GRID_H - ATT_KR)
        k0 = pl.multiple_of(start0 * GRID_W, GRID_W)
        var = jnp.where(i == 0, 0, jnp.where(i == nblk - 1, 2, 1))
        s_loc = lax.dot_general(qr_s[pl.ds(q0, QB), :], kr_s[pl.ds(k0, KW), :], tb,
                                preferred_element_type=F32) + bias_ref[var]
        s_ctx = lax.dot_general(qp_s[pl.ds(q0, QB), :], kc, tb, preferred_element_type=F32)
        m = jnp.maximum(jnp.max(s_loc, axis=-1, keepdims=True), jnp.max(s_ctx, axis=-1, keepdims=True))
        p_loc = jnp.exp(s_loc - m)
        p_ctx = jnp.exp(s_ctx - m)
        den = jnp.sum(p_loc, axis=-1, keepdims=True) + jnp.sum(p_ctx, axis=-1, keepdims=True)
        o = (jnp.dot(p_loc.astype(BF16), v_s[pl.ds(k0, KW), :], preferred_element_type=F32)
             + jnp.dot(p_ctx.astype(BF16), vc, preferred_element_type=F32))
        o_ref[pl.ds(q0, QB), :] = (o / den).astype(o_ref.dtype)
        return carry

    lax.fori_loop(0, nblk, body, 0)


def _attn(p, pc, c_rowblk, c_kcol, c_vcol, cos, sin, bias):
    qc, kc_, vc_ = 3 * MIX // DH, 4 * MIX // DH, 5 * MIX // DH
    lat = lambda c: pl.BlockSpec((SEQ, DH), lambda h, b: (b, c + h))
    ctx = lambda c: pl.BlockSpec((CTX, DH), lambda h, b: (c_rowblk + b, c + h))
    tab = pl.BlockSpec((SEQ, DH), lambda h, b: (0, 0))
    QB, KW = ATT_RQ * GRID_W, ATT_KR * GRID_W
    return pl.pallas_call(
        _attn_kernel, name="na_attention",
        out_shape=jax.ShapeDtypeStruct((N_LAT, MIX), BF16),
        grid=(HEADS, BATCH),
        in_specs=[lat(qc), lat(kc_), lat(vc_), ctx(c_kcol), ctx(c_vcol), tab, tab,
                  pl.BlockSpec((None, 3, QB, KW), lambda h, b: (h, 0, 0, 0))],
        out_specs=pl.BlockSpec((SEQ, DH), lambda h, b: (b, h)),
        scratch_shapes=[pltpu.VMEM((SEQ, DH), BF16)] * 4,
        compiler_params=_cp(("arbitrary", "arbitrary"), 40),
    )(p, p, p, pc, pc, cos, sin, bias)


def _ctxattn_kernel(q_ref, k_ref, v_ref, o_ref):
    scale = DH ** -0.5
    q = (q_ref[...].astype(F32) * scale).astype(BF16)
    s = lax.dot_general(q, k_ref[...].astype(BF16), (((1,), (1,)), ((), ())), preferred_element_type=F32)
    m = jnp.max(s, axis=-1, keepdims=True)
    e = jnp.exp(s - m)
    den = jnp.sum(e, axis=-1, keepdims=True)
    o = jnp.dot(e.astype(BF16), v_ref[...].astype(BF16), preferred_element_type=F32)
    o_ref[...] = (o / den).astype(o_ref.dtype)


def _ctxattn(p):
    rb = N_LAT // CTX
    qc, kc_, vc_ = 3 * MIX // DH, 4 * MIX // DH, 5 * MIX // DH
    blk = lambda c: pl.BlockSpec((CTX, DH), lambda b, h: (rb + b, c + h))
    return pl.pallas_call(
        _ctxattn_kernel, name="ctx_attention",
        out_shape=jax.ShapeDtypeStruct((N_CTX, MIX), BF16),
        grid=(BATCH, HEADS),
        in_specs=[blk(qc), blk(kc_), blk(vc_)],
        out_specs=pl.BlockSpec((CTX, DH), lambda b, h: (b, h)),
        compiler_params=_cp(("arbitrary", "arbitrary"), 32),
    )(p, p, p)


def _pad2(a, r, c):
    return jnp.pad(a, ((0, r - a.shape[0]), (0, c - a.shape[1])))


def kernel(x, c, ctx, c_ctx, ada_w, ada_b, norm_g, ffn1_in, ffn1_out, ffn2_in, ffn2_out, w_in, pool_w, pool_scale, conv_dw_w, conv_dw_b, conv_ln_g, conv_ln_b, na_rpb, hy_short_w, hy_short_b, hy_w1, hy_b1, hy_freq1, hy_w2, hy_b2, hy_freq2, hy_w3, hy_decay, hy_bias, w_branch, w_out):
    X = jnp.concatenate([x.reshape(N_LAT, D), ctx.reshape(N_CTX, D)], axis=0)
    cc = jnp.concatenate([c, c_ctx[None], jnp.zeros((8 - BATCH - 1, D), F32)], axis=0)
    g3 = norm_g.reshape(DEPTH * 6, 1, D)
    ada_b3 = ada_b.reshape(DEPTH, 1, N_ADA * D)
    vec3 = lambda a: a.reshape(DEPTH, 1, a.shape[-1])
    pool_scale3, conv_dw_b3, conv_ln_g3, conv_ln_b3 = map(vec3, (pool_scale, conv_dw_b, conv_ln_g, conv_ln_b))
    hy_short_b3, hy_decay3 = vec3(hy_short_b), vec3(hy_decay)
    hy_bias4 = hy_bias.reshape(DEPTH, 2, 1, MIX)

    cos, sin = _rope_tables()
    bands = {SEQ: _pool_bands(SEQ), CTX: _pool_bands(CTX)}
    dft = {SEQ: _dft_mats(SEQ), CTX: _dft_mats(CTX)}
    feats = {SEQ: _hy_feats(SEQ), CTX: _hy_feats(CTX)}

    m3 = [_ada(cc, ada_w, ada_b3, l).reshape(8, 1, N_ADA * D) for l in range(DEPTH)]

    rows = N_ALL
    h = _normmod(X, g3, m3[0], 0, 0, 0, rows)
    for l in range(DEPTH):
        last = l == DEPTH - 1
        m = m3[l]

        hid = _ffn_up(h, ffn1_in, l, rows)
        y = _matmul(hid, ffn1_out, l, rows, D, BF16, name="ffn_down", ni=8, bn=512)
        X, h = _resid(X, y, g3, m, l, 1, 2, 0.5, rows, nxt=(g3, m, l, 2, 3))

        if not last:
            p = _matmul(h, w_in, l, rows, PROJ_W, P_DT, name="proj_in", ni=8, bn=1024)
            pc, c_rowblk, c_kcol, c_vcol = p, N_LAT // CTX, 4 * MIX // DH, 5 * MIX // DH
        else:
            p = _matmul(h, w_in, l, N_LAT, PROJ_W, P_DT, name="proj_in", ni=8, bn=1024)
            pc = _matmul(h, w_in, l, N_CTX, 2 * MIX, P_DT, name="proj_ctx_kv", ni=2, bn=1024,
                         row_off=N_LAT, col_off=4 * MIX)
            c_rowblk, c_kcol, c_vcol = 0, 0, MIX // DH

        fpar = (_pad2(hy_w1[l], 128, 128), _pad2(hy_b1[l][None], 1, 128), _pad2(hy_freq1[l][None], 1, 128),
                _pad2(hy_w2[l], 128, 128), _pad2(hy_b2[l][None], 1, 128), _pad2(hy_freq2[l][None], 1, 128),
                _pad2(hy_w3[l], 128, 4 * MIX))

        def mixers(S, nseq, row_off):
            F, G = dft[S]
            KF = _kf(F, _hyfilt(feats[S], *fpar, hy_decay3, l, S), S)
            y0 = _pool(p, bands[S], pool_w, pool_scale3, l, S, nseq, row_off)
            y1 = _conv(p, conv_dw_w, conv_dw_b3, conv_ln_g3, conv_ln_b3, l, S, nseq, row_off)
            y3 = _hyena(p, F, G, KF, hy_short_w, hy_short_b3, hy_bias4, l, S, nseq, row_off)
            return y0, y1, y3

        y0, y1, y3 = mixers(SEQ, BATCH, 0)
        y2 = _attn(p, pc, c_rowblk, c_kcol, c_vcol, cos, sin, _attn_bias(na_rpb[l]))
        if not last:
            c0, c1, c3 = mixers(CTX, BATCH, N_LAT)
            c2 = _ctxattn(p)
            ys = [jnp.concatenate(pair, axis=0) for pair in ((y0, c0), (y1, c1), (y2, c2), (y3, c3))]
            mrows = rows
        else:
            ys = [y0, y1, y2, y3]
            mrows = N_LAT
        merged = _branch(ys, p, w_branch, l, mrows)
        y = _matmul(merged, w_out, l, mrows, D, BF16, name="proj_out", ni=8, bn=1024)
        X, h = _resid(X, y, g3, m, l, 3, 5, 1.0, mrows, nxt=(g3, m, l, 4, 6))
        rows = mrows

        hid = _ffn_up(h, ffn2_in, l, rows)
        y = _matmul(hid, ffn2_out, l, rows, D, BF16, name="ffn_down", ni=8, bn=512)
        if not last:
            X, h = _resid(X, y, g3, m, l, 5, 8, 0.5, rows, nxt=(g3, m3[l + 1], l + 1, 0, 0))
        else:
            X = _resid(X, y, g3, m, l, 5, 8, 0.5, rows)
    return X.reshape(BATCH, SEQ, D)
```

```python
import functools
import math

import numpy as np
import jax
import jax.numpy as jnp
from jax import lax
from jax.experimental import pallas as pl
from jax.experimental.pallas import tpu as pltpu

F32 = jnp.float32
BF16 = jnp.bfloat16

D = 4096
BATCH = 4
SEQ = 2048
DEPTH = 2
GRID_W = 64
GRID_H = SEQ // GRID_W
CTX = 256
D_FF = 6144
MIX = D // 4
N_BRANCH = 4
POOL_WINDOWS = (2, 4, 8, 16)
POOL_G = MIX // len(POOL_WINDOWS)
CONV_K = 31
DH = 128
HEADS = MIX // DH
WIN_R = 8
WIN_C = 16
ROPE_BASE = 10000.0
HY_EMB = 33
HY_HID = 64
N_ADA = 9
EPS = 1e-6
PROJ_W = 9 * MIX + N_BRANCH * D
G_OFF = 9 * MIX

N_LAT = BATCH * SEQ
N_CTX = BATCH * CTX
N_ALL = N_LAT + N_CTX

P_DT = BF16
NEG = -1e30

ATT_RQ = 4
ATT_KR = 12
MIB = 1024 * 1024
STREAM_VMEM_MIB = 60


def _cp(sem, vmem_mib):
    return pltpu.CompilerParams(dimension_semantics=sem, vmem_limit_bytes=vmem_mib * MIB)


def _modrow(i, bm):
    return jnp.minimum((i * bm) // SEQ, BATCH)


def _silu(v):
    return v * jax.nn.sigmoid(v)


def _ada_kernel(c_ref, w_ref, b_ref, o_ref):
    s = _silu(c_ref[...]).astype(BF16)
    o_ref[...] = jnp.dot(s, w_ref[...].astype(BF16), preferred_element_type=F32) + b_ref[...]


def _ada(cc, ada_w, ada_b3, l):
    bn = 512
    return pl.pallas_call(
        _ada_kernel, name="ada",
        out_shape=jax.ShapeDtypeStruct((8, N_ADA * D), F32),
        grid=(N_ADA * D // bn,),
        in_specs=[pl.BlockSpec((8, D), lambda j: (0, 0)),
                  pl.BlockSpec((None, D, bn), lambda j: (l, 0, j)),
                  pl.BlockSpec((None, 1, bn), lambda j: (l, 0, j))],
        out_specs=pl.BlockSpec((8, bn), lambda j: (0, j)),
        compiler_params=_cp(("arbitrary",), 40),
    )(cc, ada_w, ada_b3)


def _rms(v, g):
    return v * lax.rsqrt(jnp.mean(v * v, axis=-1, keepdims=True) + EPS) * g


def _stream_rows(lat_ref, ctx_ref):
    bm = lat_ref.shape[0]
    return jnp.where(pl.program_id(0) * bm < N_LAT, lat_ref[...], ctx_ref[...])


def _stream_specs(bm):
    nl = N_LAT // bm
    return [pl.BlockSpec((bm, D), lambda i: (jnp.minimum(i, nl - 1), 0)),
            pl.BlockSpec((bm, D), lambda i: (jnp.maximum(i - nl, 0), 0))]


def _normmod_kernel(x_ref, c_ref, g_ref, sh_ref, sc_ref, h_ref):
    y = _rms(_stream_rows(x_ref, c_ref), g_ref[...])
    h_ref[...] = (y * (1.0 + sc_ref[...]) + sh_ref[...]).astype(h_ref.dtype)


def _mod_spec(bm, k):
    return pl.BlockSpec((None, 1, D), lambda i: (_modrow(i, bm), 0, k))


def _g_spec(l, k):
    return pl.BlockSpec((None, 1, D), lambda i: (l * 6 + k, 0, 0))


def _normmod(x_lat, x_ctx, g3, m3, l, gk, mk):
    bm = 256
    return pl.pallas_call(
        _normmod_kernel, name="normmod",
        out_shape=jax.ShapeDtypeStruct((N_ALL, D), BF16),
        grid=(N_ALL // bm,),
        in_specs=_stream_specs(bm) + [_g_spec(l, gk), _mod_spec(bm, mk), _mod_spec(bm, mk + 1)],
        out_specs=pl.BlockSpec((bm, D), lambda i: (i, 0)),
        compiler_params=_cp(("arbitrary",), 32),
    )(x_lat, x_ctx, g3, m3, m3)


def _resid_kernel(*refs, coef, with_next, split_x):
    if split_x:
        x = _stream_rows(refs[0], refs[1])
        y_ref, gp_ref, gate_ref, *rest = refs[2:]
    else:
        x = refs[0][...]
        y_ref, gp_ref, gate_ref, *rest = refs[1:]
    xn = x + coef * gate_ref[...] * _rms(y_ref[...].astype(F32), gp_ref[...])
    if with_next:
        gn_ref, sh_ref, sc_ref, xo_ref, h_ref = rest
        xo_ref[...] = xn
        h_ref[...] = (_rms(xn, gn_ref[...]) * (1.0 + sc_ref[...]) + sh_ref[...]).astype(h_ref.dtype)
    else:
        (xo_ref,) = rest
        xo_ref[...] = xn


def _resid(x, y, g3, m3, l, gk_post, mk_gate, coef, rows, nxt=None):
    bm = 256
    split_x = isinstance(x, tuple)
    x_specs = _stream_specs(bm) if split_x else [pl.BlockSpec((bm, D), lambda i: (i, 0))]
    in_specs = x_specs + [pl.BlockSpec((bm, D), lambda i: (i, 0)), _g_spec(l, gk_post), _mod_spec(bm, mk_gate)]
    args = (list(x) if split_x else [x]) + [y, g3, m3]
    out_shape = [jax.ShapeDtypeStruct((rows, D), F32)]
    out_specs = [pl.BlockSpec((bm, D), lambda i: (i, 0))]
    if nxt is not None:
        g3n, m3n, ln, gkn, mkn = nxt
        in_specs += [_g_spec(ln, gkn), _mod_spec(bm, mkn), _mod_spec(bm, mkn + 1)]
        args += [g3n, m3n, m3n]
        out_shape.append(jax.ShapeDtypeStruct((rows, D), BF16))
        out_specs.append(pl.BlockSpec((bm, D), lambda i: (i, 0)))
    res = pl.pallas_call(
        functools.partial(_resid_kernel, coef=coef, with_next=nxt is not None, split_x=split_x), name="resid",
        out_shape=out_shape, grid=(rows // bm,), in_specs=in_specs, out_specs=out_specs,
        compiler_params=_cp(("arbitrary",), 48),
    )(*args)
    return res if nxt is not None else res[0]


def _stream_kernel(*refs, n_x, n_w, n_e, combine):
    xs = refs[:n_x]
    ws = refs[n_x:n_x + n_w]
    es = refs[n_x + n_w:n_x + n_w + n_e]
    o_ref = refs[n_x + n_w + n_e]
    slots = refs[n_x + n_w + n_e + 1:]
    slot_a, slot_b = slots[:n_w], slots[n_w:]
    j, i = pl.program_id(0), pl.program_id(1)

    def stage(dst):
        for w_ref, d in zip(ws, dst):
            ck = w_ref.shape[-2]
            r0 = pl.multiple_of(i * ck, ck)
            if len(w_ref.shape) == 3:
                d[:, pl.ds(r0, ck), :] = w_ref[...].astype(BF16)
            else:
                d[pl.ds(r0, ck), :] = w_ref[...].astype(BF16)

    def compute(src):
        o_ref[...] = combine(xs, src, es).astype(o_ref.dtype)

    @pl.when(j == 0)
    def _():
        stage(slot_a)
        o_ref[...] = jnp.zeros(o_ref.shape, o_ref.dtype)

    @pl.when(j % 2 == 1)
    def _():
        stage(slot_b)
        compute(slot_a)

    @pl.when((j > 0) & (j % 2 == 0))
    def _():
        stage(slot_a)
        compute(slot_b)


def _comb_plain(xs, ws, es):
    return jnp.dot(xs[0][...], ws[0][...], preferred_element_type=F32)


def _comb_swiglu(xs, ws, es):
    x = xs[0][...]
    a = jnp.dot(x, ws[0][...], preferred_element_type=F32)
    b = jnp.dot(x, ws[1][...], preferred_element_type=F32)
    return _silu(a) * b


def _comb_branch(xs, ws, es):
    acc = None
    for br in range(N_BRANCH):
        t = jax.nn.sigmoid(es[br][...].astype(F32)) * jnp.dot(xs[br][...], ws[0][br], preferred_element_type=F32)
        acc = t if acc is None else acc + t
    return acc


def _row_idx(j, i):
    return jnp.where(j == 0, 0, i)


def _stream_call(name, combine, xs, x_k, ws, w_specs, w_slots, es, e_specs, rows, n_out, out_dtype,
                 bm, bn, ro, vmem):
    nj, ni = n_out // bn, rows // bm
    x_specs = [pl.BlockSpec((bm, x_k), lambda j, i: (_row_idx(j, i) + ro, 0))] * len(xs)
    return pl.pallas_call(
        functools.partial(_stream_kernel, n_x=len(xs), n_w=len(ws), n_e=len(es), combine=combine), name=name,
        out_shape=jax.ShapeDtypeStruct((rows, n_out), out_dtype),
        grid=(nj + 1, ni),
        in_specs=x_specs + w_specs + e_specs,
        out_specs=pl.BlockSpec((bm, bn), lambda j, i: (_row_idx(j, i), jnp.maximum(j - 1, 0))),
        scratch_shapes=w_slots + w_slots,
        compiler_params=_cp(("arbitrary", "arbitrary"), vmem),
    )(*xs, *ws, *es)


def _matmul(x, w, l, rows, n_out, out_dtype, *, name, ni, bn, row_off=0, col_off=0, vmem=STREAM_VMEM_MIB):
    K = x.shape[1]
    bm, ck, nj = rows // ni, K // ni, n_out // bn
    co = col_off // bn
    w_specs = [pl.BlockSpec((None, ck, bn), lambda j, i: (l, i, jnp.minimum(j, nj - 1) + co))]
    return _stream_call(name, _comb_plain, [x], K, [w], w_specs, [pltpu.VMEM((K, bn), BF16)], [], [],
                        rows, n_out, out_dtype, bm, bn, row_off // bm, vmem)


def _ffn_up(h, w_up, l, rows):
    ni, bn = 8, 512
    bm, ck, nj = rows // ni, D // ni, D_FF // bn
    w_specs = [pl.BlockSpec((None, ck, bn), lambda j, i: (l, i, jnp.minimum(j, nj - 1))),
               pl.BlockSpec((None, ck, bn), lambda j, i: (l, i, jnp.minimum(j, nj - 1) + nj))]
    return _stream_call("ffn_up", _comb_swiglu, [h], D, [w_up, w_up], w_specs, [pltpu.VMEM((D, bn), BF16)] * 2,
                        [], [], rows, D_FF, BF16, bm, bn, 0, STREAM_VMEM_MIB)


def _branch(ys, p, w_branch, l, rows):
    ni, bn = 8, 512
    bm, ck, nj = rows // ni, MIX // ni, D // bn
    w_specs = [pl.BlockSpec((None, N_BRANCH, ck, bn), lambda j, i: (l, 0, i, jnp.minimum(j, nj - 1)))]
    e_specs = [pl.BlockSpec((bm, bn), functools.partial(
        lambda j, i, br: (_row_idx(j, i), (G_OFF + br * D) // bn + jnp.maximum(j - 1, 0)), br=br))
        for br in range(N_BRANCH)]
    return _stream_call("branch_merge", _comb_branch, list(ys), MIX, [w_branch], w_specs,
                        [pltpu.VMEM((N_BRANCH, MIX, bn), BF16)], [p] * N_BRANCH, e_specs,
                        rows, D, BF16, bm, bn, 0, STREAM_VMEM_MIB)


def _pool_bands(S):
    t = jnp.arange(S, dtype=jnp.int32)[:, None]
    m = jnp.arange(S, dtype=jnp.int32)[None, :]
    bands = []
    for w in POOL_WINDOWS:
        lo = jnp.clip(t - w // 2, 0, S)
        hi = jnp.clip(t + w // 2, 0, S)
        inside = (m >= lo) & (m < hi)
        band = jnp.where(inside, 1.0 / (hi - lo).astype(F32), 0.0) - (m == t).astype(F32)
        bands.append(band)
    return jnp.stack(bands).astype(BF16)


def _pool_kernel(band_ref, p_ref, w_ref, sc_ref, o_ref):
    y = jnp.dot(band_ref[...], p_ref[...].astype(BF16), preferred_element_type=F32)
    z = jnp.dot(y.astype(BF16), w_ref[...].astype(BF16), preferred_element_type=F32)
    o_ref[...] = (z * sc_ref[...]).astype(o_ref.dtype)


def _pool(p, bands, pool_w, pool_scale3, l, S, nseq, row_off):
    ro = row_off // S
    ng = len(POOL_WINDOWS)
    return pl.pallas_call(
        _pool_kernel, name="pool",
        out_shape=jax.ShapeDtypeStruct((nseq * S, MIX), BF16),
        grid=(ng, nseq),
        in_specs=[pl.BlockSpec((None, S, S), lambda g, s: (g, 0, 0)),
                  pl.BlockSpec((S, POOL_G), lambda g, s: (s + ro, g)),
                  pl.BlockSpec((None, None, POOL_G, POOL_G), lambda g, s: (l, g, 0, 0)),
                  pl.BlockSpec((None, 1, POOL_G), lambda g, s: (l, 0, g))],
        out_specs=pl.BlockSpec((S, POOL_G), lambda g, s: (s, g)),
        compiler_params=_cp(("arbitrary", "arbitrary"), 40),
    )(bands, p, pool_w, pool_scale3)


CONV_PAD = 16
CONV_CH = 32


def _conv_kernel(a_ref, g_ref, w_ref, b_ref, lg_ref, lb_ref, o_ref, u_ref, *, S):
    C = a_ref.shape[1]
    zeros = jnp.zeros((CONV_PAD, C), F32)
    u_ref[0:CONV_PAD, :] = zeros
    u_ref[S + CONV_PAD:S + 2 * CONV_PAD, :] = zeros
    SC = 128

    def stage(i, carry):
        r0 = pl.multiple_of(i * SC, SC)
        a = a_ref[pl.ds(r0, SC), :].astype(F32)
        g = g_ref[pl.ds(r0, SC), :].astype(F32)
        u_ref[pl.ds(r0 + CONV_PAD, SC), :] = a * jax.nn.sigmoid(g)
        return carry

    lax.fori_loop(0, S // SC, stage, 0)

    n = CONV_CH + 2 * CONV_PAD
    off = CONV_PAD - CONV_K // 2

    def body(i, carry):
        r0 = pl.multiple_of(i * CONV_CH, CONV_CH)
        win = u_ref[pl.ds(r0, n), :]
        acc = jnp.broadcast_to(b_ref[...], (CONV_CH, C))
        for sub in range(8):
            wsub = win if sub == 0 else pltpu.roll(win, n - sub, axis=0)
            for al in range(n // 8):
                k = 8 * al + sub - off
                if 0 <= k < CONV_K:
                    acc = acc + w_ref[k:k + 1, :] * wsub[8 * al:8 * al + CONV_CH]
        mu = jnp.mean(acc, axis=-1, keepdims=True)
        xc = acc - mu
        var = jnp.mean(xc * xc, axis=-1, keepdims=True)
        y = xc * lax.rsqrt(var + EPS) * lg_ref[...] + lb_ref[...]
        o_ref[pl.ds(r0, CONV_CH), :] = _silu(y).astype(o_ref.dtype)
        return carry

    lax.fori_loop(0, S // CONV_CH, body, 0)


def _conv(p, dw_w, dw_b3, ln_g3, ln_b3, l, S, nseq, row_off):
    ro = row_off // S
    vec = lambda: pl.BlockSpec((None, 1, MIX), lambda s: (l, 0, 0))
    return pl.pallas_call(
        functools.partial(_conv_kernel, S=S), name="conv_module",
        out_shape=jax.ShapeDtypeStruct((nseq * S, MIX), BF16),
        grid=(nseq,),
        in_specs=[pl.BlockSpec((S, MIX), lambda s: (s + ro, 1)),
                  pl.BlockSpec((S, MIX), lambda s: (s + ro, 2)),
                  pl.BlockSpec((None, CONV_K, MIX), lambda s: (l, 0, 0)),
                  vec(), vec(), vec()],
        out_specs=pl.BlockSpec((S, MIX), lambda s: (s, 0)),
        scratch_shapes=[pltpu.VMEM((S + 2 * CONV_PAD, MIX), F32)],
        compiler_params=_cp(("arbitrary",), 48),
    )(p, p, dw_w, dw_b3, ln_g3, ln_b3)


HY_PAD = 8
HY_CH = 64


def _hyshort_kernel(p_ref, w_ref, b_ref, o_ref, u_ref, *, S):
    C = p_ref.shape[1]
    zeros = jnp.zeros((HY_PAD, C), F32)
    u_ref[0:HY_PAD, :] = zeros
    u_ref[S + HY_PAD:S + 2 * HY_PAD, :] = zeros
    SC = 128

    def stage(i, carry):
        r0 = pl.multiple_of(i * SC, SC)
        u_ref[pl.ds(r0 + HY_PAD, SC), :] = p_ref[pl.ds(r0, SC), :].astype(F32)
        return carry

    lax.fori_loop(0, S // SC, stage, 0)
    n = HY_CH + 2 * HY_PAD

    def body(i, carry):
        r0 = pl.multiple_of(i * HY_CH, HY_CH)
        win = u_ref[pl.ds(r0, n), :]
        prev = pltpu.roll(win, n - (HY_PAD - 1), axis=0)[0:HY_CH]
        mid = win[HY_PAD:HY_PAD + HY_CH]
        nxt = pltpu.roll(win, n - 1, axis=0)[HY_PAD:HY_PAD + HY_CH]
        o_ref[pl.ds(r0, HY_CH), :] = (w_ref[0:1, :] * prev + w_ref[1:2, :] * mid + w_ref[2:3, :] * nxt
                                       + b_ref[...])
        return carry

    lax.fori_loop(0, S // HY_CH, body, 0)


def _hyshort(p, sw, sb3, l, S, nseq, row_off):
    ro = row_off // S
    cb = (6 * MIX) // MIX
    return pl.pallas_call(
        functools.partial(_hyshort_kernel, S=S), name="hy_short",
        out_shape=jax.ShapeDtypeStruct((nseq * S, 3 * MIX), F32),
        grid=(nseq, 3),
        in_specs=[pl.BlockSpec((S, MIX), lambda s, c: (s + ro, cb + c)),
                  pl.BlockSpec((None, 3, MIX), lambda s, c: (l, 0, c)),
                  pl.BlockSpec((None, 1, MIX), lambda s, c: (l, 0, c))],
        out_specs=pl.BlockSpec((S, MIX), lambda s, c: (s, c)),
        scratch_shapes=[pltpu.VMEM((S + 2 * HY_PAD, MIX), F32)],
        compiler_params=_cp(("arbitrary", "arbitrary"), 48),
    )(p, sw, sb3)


def _hy_feats(L):
    t = jnp.linspace(0.0, 1.0, L, dtype=F32)[:, None]
    omega = (2.0 * math.pi / L) * jnp.arange(L, dtype=F32)[:, None]
    bands = (HY_EMB - 1) // 2
    freqs = jnp.linspace(1e-4, bands - 1, bands, dtype=F32)[None, :]
    z = jnp.concatenate([t, jnp.cos(freqs * omega), -jnp.sin(freqs * omega)], axis=-1)
    return jnp.pad(z, ((0, 0), (0, 128 - HY_EMB)))


HYF_CH = 256


def _hyfilt_kernel(z_ref, w1_ref, b1_ref, f1_ref, w2_ref, b2_ref, f2_ref, w3_ref, dec_ref, o_ref, *, L):
    hp = lax.Precision.HIGHEST
    h = jnp.sin(f1_ref[...] * (jnp.dot(z_ref[...], w1_ref[...], precision=hp, preferred_element_type=F32)
                               + b1_ref[...]))
    h = jnp.sin(f2_ref[...] * (jnp.dot(h, w2_ref[...], precision=hp, preferred_element_type=F32) + b2_ref[...]))
    o = jnp.dot(h, w3_ref[...], precision=hp, preferred_element_type=F32)
    r0 = pl.program_id(0) * HYF_CH
    t = (lax.broadcasted_iota(jnp.int32, o.shape, 0) + r0).astype(F32) * (1.0 / (L - 1))
    o_ref[...] = o * jnp.exp(-t * jnp.abs(dec_ref[...]))


def _hyfilt(zf, w1p, b1p, f1p, w2p, b2p, f2p, w3p, dec3, l, L):
    nfc = 4 * MIX
    small = lambda shp: pl.BlockSpec(shp, lambda i: (0,) * len(shp))
    return pl.pallas_call(
        functools.partial(_hyfilt_kernel, L=L), name="hy_filter",
        out_shape=jax.ShapeDtypeStruct((L, nfc), F32),
        grid=(L // HYF_CH,),
        in_specs=[pl.BlockSpec((HYF_CH, 128), lambda i: (i, 0)), small((128, 128)), small((1, 128)), small((1, 128)),
                  small((128, 128)), small((1, 128)), small((1, 128)), small((128, nfc)),
                  pl.BlockSpec((None, 1, nfc), lambda i: (l, 0, 0))],
        out_specs=pl.BlockSpec((HYF_CH, nfc), lambda i: (i, 0)),
        compiler_params=_cp(("arbitrary",), 40),
    )(zf, w1p, b1p, f1p, w2p, b2p, f2p, w3p, dec3)


def _dft_mats(L):
    N = 2 * L
    a = jnp.arange(L, dtype=jnp.int32)
    idx = (a[:, None] * a[None, :]) & (N - 1)
    ang = idx.astype(F32) * (2.0 * math.pi / N)
    cs, sn = jnp.cos(ang), jnp.sin(ang)
    alt = (1 - 2 * (a & 1)).astype(F32)
    first = (a == 0)
    fs = jnp.where(first[:, None], alt[None, :], -sn)
    F = jnp.stack([cs, fs]).astype(BF16)
    return F, F[1].T


def _kf_kernel(fc_ref, fs_ref, hf_ref, hb_ref, o_ref, hs_s, hd_s, nyq_s):
    @pl.when(pl.program_id(2) == 0)
    def _():
        hf = hf_ref[...]
        rows = lax.broadcasted_iota(jnp.int32, hf.shape, 0)
        hb = jnp.where(rows == 0, 0.0, hb_ref[...])
        hs = hf + hb
        hs_s[...] = hs.astype(BF16)
        hd_s[...] = (hf - hb).astype(BF16)
        alt = (1 - 2 * (rows & 1)).astype(F32)
        nyq_s[...] = jnp.broadcast_to(jnp.sum(hs * alt, axis=0, keepdims=True), nyq_s.shape)

    kr = jnp.dot(fc_ref[...], hs_s[...], preferred_element_type=F32)
    ki = jnp.dot(fs_ref[...], hd_s[...], preferred_element_type=F32)
    bf = kr.shape[0]
    frow = lax.broadcasted_iota(jnp.int32, kr.shape, 0) + pl.program_id(2) * bf
    o_ref[0] = kr
    o_ref[1] = jnp.where(frow == 0, nyq_s[0:1, :], ki)


def _kf(F, hfilt, L):
    bf = min(512, L)
    cw = 512
    ncb = MIX // cw
    return pl.pallas_call(
        _kf_kernel, name="hy_filter_dft",
        out_shape=jax.ShapeDtypeStruct((2, 2, L, MIX), F32),
        grid=(2, ncb, L // bf),
        in_specs=[pl.BlockSpec((None, bf, L), lambda o, c, f: (0, f, 0)),
                  pl.BlockSpec((None, bf, L), lambda o, c, f: (1, f, 0)),
                  pl.BlockSpec((L, cw), lambda o, c, f: (0, o * ncb + c)),
                  pl.BlockSpec((L, cw), lambda o, c, f: (0, (2 + o) * ncb + c))],
        out_specs=pl.BlockSpec((None, 2, bf, cw), lambda o, c, f: (o, 0, f, c)),
        scratch_shapes=[pltpu.VMEM((L, cw), BF16)] * 2 + [pltpu.VMEM((8, cw), F32)],
        compiler_params=_cp(("arbitrary", "arbitrary", "arbitrary"), 48),
    )(F, F, hfilt, hfilt)


def _hyfwd_kernel(z_ref, fc_ref, fs_ref, kr_ref, ki_ref, o_ref, zbf_ref):
    @pl.when(pl.program_id(1) == 0)
    def _():
        zbf_ref[...] = z_ref[...].astype(BF16)

    zb = zbf_ref[...]
    zr = jnp.dot(fc_ref[...], zb, preferred_element_type=F32)
    zi = jnp.dot(fs_ref[...], zb, preferred_element_type=F32)
    kr, ki = kr_ref[...], ki_ref[...]
    bf = zr.shape[0]
    frow = lax.broadcasted_iota(jnp.int32, zr.shape, 0) + pl.program_id(1) * bf
    dc = frow == 0
    n_circ = 2 * zb.shape[0]
    wf = jnp.where(dc, 1.0 / n_circ, 2.0 / n_circ)
    o_ref[0] = (jnp.where(dc, zr * kr, zr * kr - zi * ki) * wf).astype(o_ref.dtype)
    o_ref[1] = (jnp.where(dc, zi * ki, zr * ki + zi * kr) * wf).astype(o_ref.dtype)


def _hyfwd(z, zcol, F, KF, o, S, nseq):
    bf = min(512, S)
    return pl.pallas_call(
        _hyfwd_kernel, name="hy_fwd_dft",
        out_shape=jax.ShapeDtypeStruct((nseq, 2, S, MIX), BF16),
        grid=(nseq, S // bf),
        in_specs=[pl.BlockSpec((S, MIX), lambda s, f: (s, zcol)),
                  pl.BlockSpec((None, bf, S), lambda s, f: (0, f, 0)),
                  pl.BlockSpec((None, bf, S), lambda s, f: (1, f, 0)),
                  pl.BlockSpec((None, None, bf, MIX), lambda s, f: (o, 0, f, 0)),
                  pl.BlockSpec((None, None, bf, MIX), lambda s, f: (o, 1, f, 0))],
        out_specs=pl.BlockSpec((None, 2, bf, MIX), lambda s, f: (s, 0, f, 0)),
        scratch_shapes=[pltpu.VMEM((S, MIX), BF16)],
        compiler_params=_cp(("arbitrary", "arbitrary"), 48),
    )(z, F, F, KF, KF)


def _hyinv_kernel(gc_ref, gs_ref, pr_ref, pi_ref, z_ref, gate_ref, skip_ref, o_ref):
    y = (jnp.dot(gc_ref[...], pr_ref[...], preferred_element_type=F32)
         + jnp.dot(gs_ref[...], pi_ref[...], preferred_element_type=F32))
    y = y + z_ref[...] * skip_ref[...]
    o_ref[...] = (gate_ref[...] * y).astype(o_ref.dtype)


def _hyinv(F, FT, P, z, zcol, u3, gcol, hy_bias, l, o, S, nseq, out_dtype):
    bn = min(512, S)
    nb = S // bn
    return pl.pallas_call(
        _hyinv_kernel, name="hy_inv_dft",
        out_shape=jax.ShapeDtypeStruct((nseq * S, MIX), out_dtype),
        grid=(nseq, nb),
        in_specs=[pl.BlockSpec((None, bn, S), lambda s, n: (0, n, 0)),
                  pl.BlockSpec((bn, S), lambda s, n: (n, 0)),
                  pl.BlockSpec((None, None, S, MIX), lambda s, n: (s, 0, 0, 0)),
                  pl.BlockSpec((None, None, S, MIX), lambda s, n: (s, 1, 0, 0)),
                  pl.BlockSpec((bn, MIX), lambda s, n: (s * nb + n, zcol)),
                  pl.BlockSpec((bn, MIX), lambda s, n: (s * nb + n, gcol)),
                  pl.BlockSpec((None, None, 1, MIX), lambda s, n: (l, o, 0, 0))],
        out_specs=pl.BlockSpec((bn, MIX), lambda s, n: (s * nb + n, 0)),
        compiler_params=_cp(("arbitrary", "arbitrary"), 48),
    )(F, FT, P, P, z, u3, hy_bias)


def _hyena(p, F, FT, KF, hy_short_w, hy_short_b3, hy_bias4, l, S, nseq, row_off):
    u3 = _hyshort(p, hy_short_w, hy_short_b3, l, S, nseq, row_off)
    P = _hyfwd(u3, 0, F, KF, 0, S, nseq)
    z1 = _hyinv(F, FT, P, u3, 0, u3, 1, hy_bias4, l, 0, S, nseq, F32)
    P = _hyfwd(z1, 0, F, KF, 1, S, nseq)
    return _hyinv(F, FT, P, z1, 0, u3, 2, hy_bias4, l, 1, S, nseq, BF16)


def _rope_tables():
    d_axis = DH // 2
    inv = ROPE_BASE ** (-jnp.arange(0, d_axis, 2, dtype=F32) / d_axis)
    t = jnp.arange(SEQ)
    pos = jnp.stack([t // GRID_W, t % GRID_W], axis=-1).astype(F32)
    lane = np.arange(DH)
    ang = pos[:, lane // d_axis] * inv[lane % (d_axis // 2)][None, :]
    sign = np.where((lane % d_axis) < d_axis // 2, -1.0, 1.0).astype(np.float32)
    return jnp.cos(ang), jnp.sin(ang) * sign[None, :]


def _toeplitz_kernel(r_ref, e_ref, o_ref):
    o_ref[...] = jnp.dot(r_ref[...], e_ref[...], precision=lax.Precision.HIGHEST, preferred_element_type=F32)


def _attn_bias(rpb_l):
    n_dr, n_dc = 2 * WIN_R - 1, 2 * WIN_C - 1
    d = lax.broadcasted_iota(jnp.int32, (128, GRID_W * GRID_W), 0)
    cw = lax.broadcasted_iota(jnp.int32, (128, GRID_W * GRID_W), 1)
    c, w = cw // GRID_W, cw % GRID_W
    c0 = jnp.clip(c - WIN_C // 2, 0, GRID_W - WIN_C)
    col_in = (w >= c0) & (w < c0 + WIN_C)
    sel = jnp.where(col_in, d == jnp.clip(w - c + (WIN_C - 1), 0, n_dc - 1), d == n_dc).astype(F32)
    tab = jnp.concatenate([rpb_l.reshape(HEADS * n_dr, n_dc).astype(F32),
                           jnp.full((HEADS * n_dr, 1), NEG, F32)], axis=1)
    tab = jnp.pad(tab, ((0, 128 - HEADS * n_dr), (0, 128 - n_dc - 1)))
    toe = pl.pallas_call(
        _toeplitz_kernel, name="na_bias_toeplitz",
        out_shape=jax.ShapeDtypeStruct((128, GRID_W * GRID_W), F32),
    )(tab, sel)
    toe = toe[:HEADS * n_dr].reshape(HEADS, n_dr, GRID_W, GRID_W)
    masked = jnp.full((HEADS, GRID_W, GRID_W), NEG, F32)
    nblk = GRID_H // ATT_RQ
    out = []
    for blk in (0, 1, nblk - 1):
        start0 = int(np.clip(ATT_RQ * blk - WIN_R // 2, 0, GRID_H - ATT_KR))
        rows = []
        for j in range(ATT_RQ):
            r = ATT_RQ * blk + j
            rs = int(np.clip(r - WIN_R // 2, 0, GRID_H - WIN_R))
            tiles = []
            for ir in range(ATT_KR):
                kr = start0 + ir
                tiles.append(toe[:, kr - r + (WIN_R - 1)] if rs <= kr < rs + WIN_R else masked)
            rows.append(jnp.concatenate(tiles, axis=-1))
        out.append(jnp.concatenate(rows, axis=1))
    return jnp.stack(out, axis=1)


def _rope(x, cos, sin_signed):
    lane = lax.broadcasted_iota(jnp.int32, x.shape, 1)
    partner = jnp.where((lane & 32) == 0, pltpu.roll(x, DH - 32, axis=1), pltpu.roll(x, 32, axis=1))
    return x * cos + partner * sin_signed


def _attn_kernel(q_ref, k_ref, v_ref, kc_ref, vc_ref, cos_ref, sin_ref, bias_ref, o_ref,
                 qr_s, qp_s, kr_s, v_s):
    scale = DH ** -0.5
    q = q_ref[...].astype(F32) * scale
    cos, sin = cos_ref[...], sin_ref[...]
    qp_s[...] = q.astype(BF16)
    qr_s[...] = _rope(q, cos, sin).astype(BF16)
    kr_s[...] = _rope(k_ref[...].astype(F32), cos, sin).astype(BF16)
    v_s[...] = v_ref[...].astype(BF16)
    kc = kc_ref[...].astype(BF16)
    vc = vc_ref[...].astype(BF16)
    QB, KW = ATT_RQ * GRID_W, ATT_KR * GRID_W
    nblk = GRID_H // ATT_RQ
    tb = (((1,), (1,)), ((), ()))

    def body(i, carry):
        q0 = pl.multiple_of(i * QB, QB)
        start0 = jnp.clip(ATT_RQ * i - WIN_R // 2, 0, GRID_H - ATT_KR)
        k0 = pl.multiple_of(start0 * GRID_W, GRID_W)
        var = jnp.where(i == 0, 0, jnp.where(i == nblk - 1, 2, 1))
        s_loc = lax.dot_general(qr_s[pl.ds(q0, QB), :], kr_s[pl.ds(k0, KW), :], tb,
                                preferred_element_type=F32) + bias_ref[var]
        s_ctx = lax.dot_general(qp_s[pl.ds(q0, QB), :], kc, tb, preferred_element_type=F32)
        m = jnp.maximum(jnp.max(s_loc, axis=-1, keepdims=True), jnp.max(s_ctx, axis=-1, keepdims=True))
        p_loc = jnp.exp(s_loc - m)
        p_ctx = jnp.exp(s_ctx - m)
        den = jnp.sum(p_loc, axis=-1, keepdims=True) + jnp.sum(p_ctx, axis=-1, keepdims=True)
        o = (jnp.dot(p_loc.astype(BF16), v_s[pl.ds(k0, KW), :], preferred_element_type=F32)
             + jnp.dot(p_ctx.astype(BF16), vc, preferred_element_type=F32))
        o_ref[pl.ds(q0, QB), :] = (o / den).astype(o_ref.dtype)
        return carry

    lax.fori_loop(0, nblk, body, 0, unroll=2)


def _attn(p, pc, c_rowblk, c_kcol, c_vcol, cos, sin, bias):
    qc, kc_, vc_ = 3 * MIX // DH, 4 * MIX // DH, 5 * MIX // DH
    lat = lambda c: pl.BlockSpec((SEQ, DH), lambda h, b: (b, c + h))
    ctx = lambda c: pl.BlockSpec((CTX, DH), lambda h, b: (c_rowblk + b, c + h))
    tab = pl.BlockSpec((SEQ, DH), lambda h, b: (0, 0))
    QB, KW = ATT_RQ * GRID_W, ATT_KR * GRID_W
    return pl.pallas_call(
        _attn_kernel, name="na_attention",
        out_shape=jax.ShapeDtypeStruct((N_LAT, MIX), BF16),
        grid=(HEADS, BATCH),
        in_specs=[lat(qc), lat(kc_), lat(vc_), ctx(c_kcol), ctx(c_vcol), tab, tab,
                  pl.BlockSpec((None, 3, QB, KW), lambda h, b: (h, 0, 0, 0))],
        out_specs=pl.BlockSpec((SEQ, DH), lambda h, b: (b, h)),
        scratch_shapes=[pltpu.VMEM((SEQ, DH), BF16)] * 4,
        compiler_params=_cp(("arbitrary", "arbitrary"), 40),
    )(p, p, p, pc, pc, cos, sin, bias)


def _ctxattn_kernel(q_ref, k_ref, v_ref, o_ref):
    scale = DH ** -0.5
    q = (q_ref[...].astype(F32) * scale).astype(BF16)
    s = lax.dot_general(q, k_ref[...].astype(BF16), (((1,), (1,)), ((), ())), preferred_element_type=F32)
    m = jnp.max(s, axis=-1, keepdims=True)
    e = jnp.exp(s - m)
    den = jnp.sum(e, axis=-1, keepdims=True)
    o = jnp.dot(e.astype(BF16), v_ref[...].astype(BF16), preferred_element_type=F32)
    o_ref[...] = (o / den).astype(o_ref.dtype)


def _ctxattn(p):
    rb = N_LAT // CTX
    qc, kc_, vc_ = 3 * MIX // DH, 4 * MIX // DH, 5 * MIX // DH
    blk = lambda c: pl.BlockSpec((CTX, DH), lambda b, h: (rb + b, c + h))
    return pl.pallas_call(
        _ctxattn_kernel, name="ctx_attention",
        out_shape=jax.ShapeDtypeStruct((N_CTX, MIX), BF16),
        grid=(BATCH, HEADS),
        in_specs=[blk(qc), blk(kc_), blk(vc_)],
        out_specs=pl.BlockSpec((CTX, DH), lambda b, h: (b, h)),
        compiler_params=_cp(("arbitrary", "arbitrary"), 32),
    )(p, p, p)


def _pad2(a, r, c):
    return jnp.pad(a, ((0, r - a.shape[0]), (0, c - a.shape[1])))


def kernel(x, c, ctx, c_ctx, ada_w, ada_b, norm_g, ffn1_in, ffn1_out, ffn2_in, ffn2_out, w_in, pool_w, pool_scale, conv_dw_w, conv_dw_b, conv_ln_g, conv_ln_b, na_rpb, hy_short_w, hy_short_b, hy_w1, hy_b1, hy_freq1, hy_w2, hy_b2, hy_freq2, hy_w3, hy_decay, hy_bias, w_branch, w_out):
    X = (x.reshape(N_LAT, D), ctx.reshape(N_CTX, D))
    cc = jnp.concatenate([c, c_ctx[None], jnp.zeros((8 - BATCH - 1, D), F32)], axis=0)
    g3 = norm_g.reshape(DEPTH * 6, 1, D)
    ada_b3 = ada_b.reshape(DEPTH, 1, N_ADA * D)
    vec3 = lambda a: a.reshape(DEPTH, 1, a.shape[-1])
    pool_scale3, conv_dw_b3, conv_ln_g3, conv_ln_b3 = map(vec3, (pool_scale, conv_dw_b, conv_ln_g, conv_ln_b))
    hy_short_b3, hy_decay3 = vec3(hy_short_b), vec3(hy_decay)
    hy_bias4 = hy_bias.reshape(DEPTH, 2, 1, MIX)

    cos, sin = _rope_tables()
    bands = {SEQ: _pool_bands(SEQ), CTX: _pool_bands(CTX)}
    dft = {SEQ: _dft_mats(SEQ), CTX: _dft_mats(CTX)}
    feats = {SEQ: _hy_feats(SEQ), CTX: _hy_feats(CTX)}

    m3 = [_ada(cc, ada_w, ada_b3, l).reshape(8, 1, N_ADA * D) for l in range(DEPTH)]

    rows = N_ALL
    h = _normmod(X[0], X[1], g3, m3[0], 0, 0, 0)
    for l in range(DEPTH):
        last = l == DEPTH - 1
        m = m3[l]

        hid = _ffn_up(h, ffn1_in, l, rows)
        y = _matmul(hid, ffn1_out, l, rows, D, BF16, name="ffn_down", ni=8, bn=512)
        X, h = _resid(X, y, g3, m, l, 1, 2, 0.5, rows, nxt=(g3, m, l, 2, 3))

        if not last:
            p = _matmul(h, w_in, l, rows, PROJ_W, P_DT, name="proj_in", ni=8, bn=1024)
            pc, c_rowblk, c_kcol, c_vcol = p, N_LAT // CTX, 4 * MIX // DH, 5 * MIX // DH
        else:
            p = _matmul(h, w_in, l, N_LAT, PROJ_W, P_DT, name="proj_in", ni=8, bn=1024)
            pc = _matmul(h, w_in, l, N_CTX, 2 * MIX, P_DT, name="proj_ctx_kv", ni=2, bn=1024,
                         row_off=N_LAT, col_off=4 * MIX)
            c_rowblk, c_kcol, c_vcol = 0, 0, MIX // DH

        fpar = (_pad2(hy_w1[l], 128, 128), _pad2(hy_b1[l][None], 1, 128), _pad2(hy_freq1[l][None], 1, 128),
                _pad2(hy_w2[l], 128, 128), _pad2(hy_b2[l][None], 1, 128), _pad2(hy_freq2[l][None], 1, 128),
                _pad2(hy_w3[l], 128, 4 * MIX))

        def mixers(S, nseq, row_off):
            F, G = dft[S]
            KF = _kf(F, _hyfilt(feats[S], *fpar, hy_decay3, l, S), S)
            y0 = _pool(p, bands[S], pool_w, pool_scale3, l, S, nseq, row_off)
            y1 = _conv(p, conv_dw_w, conv_dw_b3, conv_ln_g3, conv_ln_b3, l, S, nseq, row_off)
            y3 = _hyena(p, F, G, KF, hy_short_w, hy_short_b3, hy_bias4, l, S, nseq, row_off)
            return y0, y1, y3

        y0, y1, y3 = mixers(SEQ, BATCH, 0)
        y2 = _attn(p, pc, c_rowblk, c_kcol, c_vcol, cos, sin, _attn_bias(na_rpb[l]))
        if not last:
            c0, c1, c3 = mixers(CTX, BATCH, N_LAT)
            c2 = _ctxattn(p)
            ys = [jnp.concatenate(pair, axis=0) for pair in ((y0, c0), (y1, c1), (y2, c2), (y3, c3))]
            mrows = rows
        else:
            ys = [y0, y1, y2, y3]
            mrows = N_LAT
        merged = _branch(ys, p, w_branch, l, mrows)
        y = _matmul(merged, w_out, l, mrows, D, BF16, name="proj_out", ni=8, bn=1024)
        X, h = _resid(X, y, g3, m, l, 3, 5, 1.0, mrows, nxt=(g3, m, l, 4, 6))
        rows = mrows

        hid = _ffn_up(h, ffn2_in, l, rows)
        y = _matmul(hid, ffn2_out, l, rows, D, BF16, name="ffn_down", ni=8, bn=512)
        if not last:
            X, h = _resid(X, y, g3, m, l, 5, 8, 0.5, rows, nxt=(g3, m3[l + 1], l + 1, 0, 0))
        else:
            X = _resid(X, y, g3, m, l, 5, 8, 0.5, rows)
    return X.reshape(BATCH, SEQ, D)
```

```python
import functools
import math

import numpy as np
import jax
import jax.numpy as jnp
from jax import lax
from jax.experimental import pallas as pl
from jax.experimental.pallas import tpu as pltpu

F32 = jnp.float32
BF16 = jnp.bfloat16

D = 4096
BATCH = 4
SEQ = 2048
DEPTH = 2
GRID_W = 64
GRID_H = SEQ // GRID_W
CTX = 256
D_FF = 6144
MIX = D // 4
N_BRANCH = 4
POOL_WINDOWS = (2, 4, 8, 16)
POOL_G = MIX // len(POOL_WINDOWS)
CONV_K = 31
DH = 128
HEADS = MIX // DH
WIN_R = 8
WIN_C = 16
ROPE_BASE = 10000.0
HY_EMB = 33
HY_HID = 64
N_ADA = 9
EPS = 1e-6
PROJ_W = 9 * MIX + N_BRANCH * D
G_OFF = 9 * MIX

N_LAT = BATCH * SEQ
N_CTX = BATCH * CTX
N_ALL = N_LAT + N_CTX

P_DT = BF16
NEG = -1e30

ATT_RQ = 4
ATT_KR = 12
MIB = 1024 * 1024
STREAM_VMEM_MIB = 60


def _cp(sem, vmem_mib):
    return pltpu.CompilerParams(dimension_semantics=sem, vmem_limit_bytes=vmem_mib * MIB)


def _modrow(i, bm):
    return jnp.minimum((i * bm) // SEQ, BATCH)


def _silu(v):
    return v * jax.nn.sigmoid(v)


def _ada_kernel(c_ref, w_ref, b_ref, o_ref):
    s = _silu(c_ref[...]).astype(BF16)
    o_ref[...] = jnp.dot(s, w_ref[...].astype(BF16), preferred_element_type=F32) + b_ref[...]


def _ada(cc, ada_w, ada_b3, l):
    bn = 512
    return pl.pallas_call(
        _ada_kernel, name="ada",
        out_shape=jax.ShapeDtypeStruct((8, N_ADA * D), F32),
        grid=(N_ADA * D // bn,),
        in_specs=[pl.BlockSpec((8, D), lambda j: (0, 0)),
                  pl.BlockSpec((None, D, bn), lambda j: (l, 0, j)),
                  pl.BlockSpec((None, 1, bn), lambda j: (l, 0, j))],
        out_specs=pl.BlockSpec((8, bn), lambda j: (0, j)),
        compiler_params=_cp(("arbitrary",), 40),
    )(cc, ada_w, ada_b3)


def _rms(v, g):
    return v * lax.rsqrt(jnp.mean(v * v, axis=-1, keepdims=True) + EPS) * g


def _stream_rows(lat_ref, ctx_ref):
    bm = lat_ref.shape[0]
    return jnp.where(pl.program_id(0) * bm < N_LAT, lat_ref[...], ctx_ref[...])


def _stream_specs(bm):
    nl = N_LAT // bm
    return [pl.BlockSpec((bm, D), lambda i: (jnp.minimum(i, nl - 1), 0)),
            pl.BlockSpec((bm, D), lambda i: (jnp.maximum(i - nl, 0), 0))]


def _normmod_kernel(x_ref, c_ref, g_ref, sh_ref, sc_ref, h_ref):
    y = _rms(_stream_rows(x_ref, c_ref), g_ref[...])
    h_ref[...] = (y * (1.0 + sc_ref[...]) + sh_ref[...]).astype(h_ref.dtype)


def _mod_spec(bm, k):
    return pl.BlockSpec((None, 1, D), lambda i: (_modrow(i, bm), 0, k))


def _g_spec(l, k):
    return pl.BlockSpec((None, 1, D), lambda i: (l * 6 + k, 0, 0))


def _normmod(x_lat, x_ctx, g3, m3, l, gk, mk):
    bm = 256
    return pl.pallas_call(
        _normmod_kernel, name="normmod",
        out_shape=jax.ShapeDtypeStruct((N_ALL, D), BF16),
        grid=(N_ALL // bm,),
        in_specs=_stream_specs(bm) + [_g_spec(l, gk), _mod_spec(bm, mk), _mod_spec(bm, mk + 1)],
        out_specs=pl.BlockSpec((bm, D), lambda i: (i, 0)),
        compiler_params=_cp(("arbitrary",), 32),
    )(x_lat, x_ctx, g3, m3, m3)


def _resid_kernel(*refs, coef, with_next, split_x):
    if split_x:
        x = _stream_rows(refs[0], refs[1])
        y_ref, gp_ref, gate_ref, *rest = refs[2:]
    else:
        x = refs[0][...]
        y_ref, gp_ref, gate_ref, *rest = refs[1:]
    xn = x + coef * gate_ref[...] * _rms(y_ref[...].astype(F32), gp_ref[...])
    if with_next:
        gn_ref, sh_ref, sc_ref, xo_ref, h_ref = rest
        xo_ref[...] = xn
        h_ref[...] = (_rms(xn, gn_ref[...]) * (1.0 + sc_ref[...]) + sh_ref[...]).astype(h_ref.dtype)
    else:
        (xo_ref,) = rest
        xo_ref[...] = xn


def _resid(x, y, g3, m3, l, gk_post, mk_gate, coef, rows, nxt=None):
    bm = 256
    split_x = isinstance(x, tuple)
    x_specs = _stream_specs(bm) if split_x else [pl.BlockSpec((bm, D), lambda i: (i, 0))]
    in_specs = x_specs + [pl.BlockSpec((bm, D), lambda i: (i, 0)), _g_spec(l, gk_post), _mod_spec(bm, mk_gate)]
    args = (list(x) if split_x else [x]) + [y, g3, m3]
    out_shape = [jax.ShapeDtypeStruct((rows, D), F32)]
    out_specs = [pl.BlockSpec((bm, D), lambda i: (i, 0))]
    if nxt is not None:
        g3n, m3n, ln, gkn, mkn = nxt
        in_specs += [_g_spec(ln, gkn), _mod_spec(bm, mkn), _mod_spec(bm, mkn + 1)]
        args += [g3n, m3n, m3n]
        out_shape.append(jax.ShapeDtypeStruct((rows, D), BF16))
        out_specs.append(pl.BlockSpec((bm, D), lambda i: (i, 0)))
    res = pl.pallas_call(
        functools.partial(_resid_kernel, coef=coef, with_next=nxt is not None, split_x=split_x), name="resid",
        out_shape=out_shape, grid=(rows // bm,), in_specs=in_specs, out_specs=out_specs,
        compiler_params=_cp(("arbitrary",), 48),
    )(*args)
    return res if nxt is not None else res[0]


def _stream_kernel(*refs, n_x, n_w, n_e, n_chunks, combine):
    xs = refs[:n_x]
    ws = refs[n_x:n_x + n_w]
    es = refs[n_x + n_w:n_x + n_w + n_e]
    o_ref = refs[n_x + n_w + n_e]
    slots = refs[n_x + n_w + n_e + 1:]
    slot_a, slot_b = slots[:n_w], slots[n_w:]
    j, i = pl.program_id(0), pl.program_id(1)

    def stage(dst):
        for w_ref, d in zip(ws, dst):
            ck = w_ref.shape[-2]
            r0 = pl.multiple_of(jnp.minimum(i, n_chunks - 1) * ck, ck)
            if len(w_ref.shape) == 3:
                d[:, pl.ds(r0, ck), :] = w_ref[...].astype(BF16)
            else:
                d[pl.ds(r0, ck), :] = w_ref[...].astype(BF16)

    def compute(src):
        o_ref[...] = combine(xs, src, es).astype(o_ref.dtype)

    @pl.when(j == 0)
    def _():
        stage(slot_a)
        o_ref[...] = jnp.zeros(o_ref.shape, o_ref.dtype)

    @pl.when(j % 2 == 1)
    def _():
        stage(slot_b)
        compute(slot_a)

    @pl.when((j > 0) & (j % 2 == 0))
    def _():
        stage(slot_a)
        compute(slot_b)


def _comb_plain(xs, ws, es):
    return jnp.dot(xs[0][...], ws[0][...], preferred_element_type=F32)


def _comb_swiglu(xs, ws, es):
    x = xs[0][...]
    a = jnp.dot(x, ws[0][...], preferred_element_type=F32)
    b = jnp.dot(x, ws[1][...], preferred_element_type=F32)
    return _silu(a) * b


def _comb_branch(xs, ws, es, n_lat_tiles=None):
    acc = None
    for br in range(N_BRANCH):
        y = xs[br][...]
        if n_lat_tiles is not None:
            y = jnp.where(pl.program_id(1) < n_lat_tiles, y, xs[N_BRANCH + br][...])
        t = jax.nn.sigmoid(es[br][...].astype(F32)) * jnp.dot(y, ws[0][br], preferred_element_type=F32)
        acc = t if acc is None else acc + t
    return acc


def _row_idx(j, i):
    return jnp.where(j == 0, 0, i)


def _stream_call(name, combine, xs, x_k, ws, w_specs, w_slots, es, e_specs, rows, n_out, out_dtype,
                 bm, bn, ro, vmem, n_chunks=None, x_specs=None):
    nj, ni = n_out // bn, rows // bm
    if x_specs is None:
        x_specs = [pl.BlockSpec((bm, x_k), lambda j, i: (_row_idx(j, i) + ro, 0))] * len(xs)
    return pl.pallas_call(
        functools.partial(_stream_kernel, n_x=len(xs), n_w=len(ws), n_e=len(es),
                          n_chunks=ni if n_chunks is None else n_chunks, combine=combine), name=name,
        out_shape=jax.ShapeDtypeStruct((rows, n_out), out_dtype),
        grid=(nj + 1, ni),
        in_specs=x_specs + w_specs + e_specs,
        out_specs=pl.BlockSpec((bm, bn), lambda j, i: (_row_idx(j, i), jnp.maximum(j - 1, 0))),
        scratch_shapes=w_slots + w_slots,
        compiler_params=_cp(("arbitrary", "arbitrary"), vmem),
    )(*xs, *ws, *es)


def _matmul(x, w, l, rows, n_out, out_dtype, *, name, ni, bn, row_off=0, col_off=0, vmem=STREAM_VMEM_MIB):
    K = x.shape[1]
    bm, ck, nj = rows // ni, K // ni, n_out // bn
    co = col_off // bn
    w_specs = [pl.BlockSpec((None, ck, bn), lambda j, i: (l, i, jnp.minimum(j, nj - 1) + co))]
    return _stream_call(name, _comb_plain, [x], K, [w], w_specs, [pltpu.VMEM((K, bn), BF16)], [], [],
                        rows, n_out, out_dtype, bm, bn, row_off // bm, vmem)


def _ffn_up(h, w_up, l, rows):
    ni, bn = 8, 512
    bm, ck, nj = rows // ni, D // ni, D_FF // bn
    w_specs = [pl.BlockSpec((None, ck, bn), lambda j, i: (l, i, jnp.minimum(j, nj - 1))),
               pl.BlockSpec((None, ck, bn), lambda j, i: (l, i, jnp.minimum(j, nj - 1) + nj))]
    return _stream_call("ffn_up", _comb_swiglu, [h], D, [w_up, w_up], w_specs, [pltpu.VMEM((D, bn), BF16)] * 2,
                        [], [], rows, D_FF, BF16, bm, bn, 0, STREAM_VMEM_MIB)


def _branch(ys, ys_ctx, p, w_branch, l, rows):
    n_chunks, bn, bm = 8, 512, N_LAT // 8
    ck, nj = MIX // n_chunks, D // bn
    w_specs = [pl.BlockSpec((None, N_BRANCH, ck, bn),
                            lambda j, i: (l, 0, jnp.minimum(i, n_chunks - 1), jnp.minimum(j, nj - 1)))]
    e_specs = [pl.BlockSpec((bm, bn), functools.partial(
        lambda j, i, br: (_row_idx(j, i), (G_OFF + br * D) // bn + jnp.maximum(j - 1, 0)), br=br))
        for br in range(N_BRANCH)]
    xs, x_specs, n_lat_tiles = list(ys), None, None
    if ys_ctx is not None:
        n_lat_tiles = N_LAT // bm
        xs = xs + list(ys_ctx)
        x_specs = ([pl.BlockSpec((bm, MIX), lambda j, i: (jnp.minimum(_row_idx(j, i), n_lat_tiles - 1), 0))] * N_BRANCH
                   + [pl.BlockSpec((bm, MIX), lambda j, i: (0, 0))] * N_BRANCH)
    return _stream_call("branch_merge", functools.partial(_comb_branch, n_lat_tiles=n_lat_tiles), xs, MIX,
                        [w_branch], w_specs, [pltpu.VMEM((N_BRANCH, MIX, bn), BF16)], [p] * N_BRANCH, e_specs,
                        rows, D, BF16, bm, bn, 0, STREAM_VMEM_MIB, n_chunks=n_chunks, x_specs=x_specs)


def _pool_bands(S):
    t = jnp.arange(S, dtype=jnp.int32)[:, None]
    m = jnp.arange(S, dtype=jnp.int32)[None, :]
    bands = []
    for w in POOL_WINDOWS:
        lo = jnp.clip(t - w // 2, 0, S)
        hi = jnp.clip(t + w // 2, 0, S)
        inside = (m >= lo) & (m < hi)
        band = jnp.where(inside, 1.0 / (hi - lo).astype(F32), 0.0) - (m == t).astype(F32)
        bands.append(band)
    return jnp.stack(bands).astype(BF16)


POOL_BLK = 256


def _pool_kernel(band_ref, p_ref, w_ref, sc_ref, o_ref):
    S = p_ref.shape[0]
    nb = S // POOL_BLK
    w = w_ref[...].astype(BF16)
    for r in range(nb):
        r0, r1 = r * POOL_BLK, (r + 1) * POOL_BLK
        k0, k1 = max(r - 1, 0) * POOL_BLK, min(r + 2, nb) * POOL_BLK
        y = jnp.dot(band_ref[r0:r1, k0:k1], p_ref[k0:k1, :].astype(BF16), preferred_element_type=F32)
        z = jnp.dot(y.astype(BF16), w, preferred_element_type=F32)
        o_ref[r0:r1, :] = (z * sc_ref[...]).astype(o_ref.dtype)


def _pool(p, bands, pool_w, pool_scale3, l, S, nseq, row_off):
    ro = row_off // S
    ng = len(POOL_WINDOWS)
    return pl.pallas_call(
        _pool_kernel, name="pool",
        out_shape=jax.ShapeDtypeStruct((nseq * S, MIX), BF16),
        grid=(ng, nseq),
        in_specs=[pl.BlockSpec((None, S, S), lambda g, s: (g, 0, 0)),
                  pl.BlockSpec((S, POOL_G), lambda g, s: (s + ro, g)),
                  pl.BlockSpec((None, None, POOL_G, POOL_G), lambda g, s: (l, g, 0, 0)),
                  pl.BlockSpec((None, 1, POOL_G), lambda g, s: (l, 0, g))],
        out_specs=pl.BlockSpec((S, POOL_G), lambda g, s: (s, g)),
        compiler_params=_cp(("arbitrary", "arbitrary"), 40),
    )(bands, p, pool_w, pool_scale3)


CONV_PAD = 16
CONV_CH = 32


def _conv_kernel(a_ref, g_ref, w_ref, b_ref, lg_ref, lb_ref, o_ref, u_ref, *, S):
    C = a_ref.shape[1]
    zeros = jnp.zeros((CONV_PAD, C), F32)
    u_ref[0:CONV_PAD, :] = zeros
    u_ref[S + CONV_PAD:S + 2 * CONV_PAD, :] = zeros
    SC = 128

    def stage(i, carry):
        r0 = pl.multiple_of(i * SC, SC)
        a = a_ref[pl.ds(r0, SC), :].astype(F32)
        g = g_ref[pl.ds(r0, SC), :].astype(F32)
        u_ref[pl.ds(r0 + CONV_PAD, SC), :] = a * jax.nn.sigmoid(g)
        return carry

    lax.fori_loop(0, S // SC, stage, 0)

    n = CONV_CH + 2 * CONV_PAD
    off = CONV_PAD - CONV_K // 2

    def body(i, carry):
        r0 = pl.multiple_of(i * CONV_CH, CONV_CH)
        win = u_ref[pl.ds(r0, n), :]
        acc = jnp.broadcast_to(b_ref[...], (CONV_CH, C))
        for sub in range(8):
            wsub = win if sub == 0 else pltpu.roll(win, n - sub, axis=0)
            for al in range(n // 8):
                k = 8 * al + sub - off
                if 0 <= k < CONV_K:
                    acc = acc + w_ref[k:k + 1, :] * wsub[8 * al:8 * al + CONV_CH]
        mu = jnp.mean(acc, axis=-1, keepdims=True)
        xc = acc - mu
        var = jnp.mean(xc * xc, axis=-1, keepdims=True)
        y = xc * lax.rsqrt(var + EPS) * lg_ref[...] + lb_ref[...]
        o_ref[pl.ds(r0, CONV_CH), :] = _silu(y).astype(o_ref.dtype)
        return carry

    lax.fori_loop(0, S // CONV_CH, body, 0)


def _conv(p, dw_w, dw_b3, ln_g3, ln_b3, l, S, nseq, row_off):
    ro = row_off // S
    vec = lambda: pl.BlockSpec((None, 1, MIX), lambda s: (l, 0, 0))
    return pl.pallas_call(
        functools.partial(_conv_kernel, S=S), name="conv_module",
        out_shape=jax.ShapeDtypeStruct((nseq * S, MIX), BF16),
        grid=(nseq,),
        in_specs=[pl.BlockSpec((S, MIX), lambda s: (s + ro, 1)),
                  pl.BlockSpec((S, MIX), lambda s: (s + ro, 2)),
                  pl.BlockSpec((None, CONV_K, MIX), lambda s: (l, 0, 0)),
                  vec(), vec(), vec()],
        out_specs=pl.BlockSpec((S, MIX), lambda s: (s, 0)),
        scratch_shapes=[pltpu.VMEM((S + 2 * CONV_PAD, MIX), F32)],
        compiler_params=_cp(("arbitrary",), 48),
    )(p, p, dw_w, dw_b3, ln_g3, ln_b3)


HY_PAD = 8
HY_CH = 64


def _hyshort_kernel(p_ref, w_ref, b_ref, o_ref, u_ref, *, S):
    C = p_ref.shape[1]
    zeros = jnp.zeros((HY_PAD, C), F32)
    u_ref[0:HY_PAD, :] = zeros
    u_ref[S + HY_PAD:S + 2 * HY_PAD, :] = zeros
    SC = 128

    def stage(i, carry):
        r0 = pl.multiple_of(i * SC, SC)
        u_ref[pl.ds(r0 + HY_PAD, SC), :] = p_ref[pl.ds(r0, SC), :].astype(F32)
        return carry

    lax.fori_loop(0, S // SC, stage, 0)
    n = HY_CH + 2 * HY_PAD

    def body(i, carry):
        r0 = pl.multiple_of(i * HY_CH, HY_CH)
        win = u_ref[pl.ds(r0, n), :]
        prev = pltpu.roll(win, n - (HY_PAD - 1), axis=0)[0:HY_CH]
        mid = win[HY_PAD:HY_PAD + HY_CH]
        nxt = pltpu.roll(win, n - 1, axis=0)[HY_PAD:HY_PAD + HY_CH]
        o_ref[pl.ds(r0, HY_CH), :] = (w_ref[0:1, :] * prev + w_ref[1:2, :] * mid + w_ref[2:3, :] * nxt
                                       + b_ref[...]).astype(o_ref.dtype)
        return carry

    lax.fori_loop(0, S // HY_CH, body, 0)


def _hyshort(p, sw, sb3, l, S, nseq, row_off):
    ro = row_off // S
    cb = (6 * MIX) // MIX
    return pl.pallas_call(
        functools.partial(_hyshort_kernel, S=S), name="hy_short",
        out_shape=jax.ShapeDtypeStruct((nseq * S, 3 * MIX), BF16),
        grid=(nseq, 3),
        in_specs=[pl.BlockSpec((S, MIX), lambda s, c: (s + ro, cb + c)),
                  pl.BlockSpec((None, 3, MIX), lambda s, c: (l, 0, c)),
                  pl.BlockSpec((None, 1, MIX), lambda s, c: (l, 0, c))],
        out_specs=pl.BlockSpec((S, MIX), lambda s, c: (s, c)),
        scratch_shapes=[pltpu.VMEM((S + 2 * HY_PAD, MIX), F32)],
        compiler_params=_cp(("arbitrary", "arbitrary"), 48),
    )(p, sw, sb3)


def _hy_feats(L):
    t = jnp.linspace(0.0, 1.0, L, dtype=F32)[:, None]
    omega = (2.0 * math.pi / L) * jnp.arange(L, dtype=F32)[:, None]
    bands = (HY_EMB - 1) // 2
    freqs = jnp.linspace(1e-4, bands - 1, bands, dtype=F32)[None, :]
    z = jnp.concatenate([t, jnp.cos(freqs * omega), -jnp.sin(freqs * omega)], axis=-1)
    return jnp.pad(z, ((0, 0), (0, 128 - HY_EMB)))


HYF_CH = 256


def _hyfilt_kernel(z_ref, w1_ref, b1_ref, f1_ref, w2_ref, b2_ref, f2_ref, w3_ref, dec_ref, o_ref, *, L):
    hp = lax.Precision.HIGHEST
    h = jnp.sin(f1_ref[...] * (jnp.dot(z_ref[...], w1_ref[...], precision=hp, preferred_element_type=F32)
                               + b1_ref[...]))
    h = jnp.sin(f2_ref[...] * (jnp.dot(h, w2_ref[...], precision=hp, preferred_element_type=F32) + b2_ref[...]))
    o = jnp.dot(h, w3_ref[...], precision=hp, preferred_element_type=F32)
    r0 = pl.program_id(0) * HYF_CH
    t = (lax.broadcasted_iota(jnp.int32, o.shape, 0) + r0).astype(F32) * (1.0 / (L - 1))
    o_ref[...] = o * jnp.exp(-t * jnp.abs(dec_ref[...]))


def _hyfilt(zf, w1p, b1p, f1p, w2p, b2p, f2p, w3p, dec3, l, L):
    nfc = 4 * MIX
    small = lambda shp: pl.BlockSpec(shp, lambda i: (0,) * len(shp))
    return pl.pallas_call(
        functools.partial(_hyfilt_kernel, L=L), name="hy_filter",
        out_shape=jax.ShapeDtypeStruct((L, nfc), F32),
        grid=(L // HYF_CH,),
        in_specs=[pl.BlockSpec((HYF_CH, 128), lambda i: (i, 0)), small((128, 128)), small((1, 128)), small((1, 128)),
                  small((128, 128)), small((1, 128)), small((1, 128)), small((128, nfc)),
                  pl.BlockSpec((None, 1, nfc), lambda i: (l, 0, 0))],
        out_specs=pl.BlockSpec((HYF_CH, nfc), lambda i: (i, 0)),
        compiler_params=_cp(("arbitrary",), 40),
    )(zf, w1p, b1p, f1p, w2p, b2p, f2p, w3p, dec3)


def _dft_mats(L):
    N = 2 * L
    a = jnp.arange(L, dtype=jnp.int32)
    idx = (a[:, None] * a[None, :]) & (N - 1)
    ang = idx.astype(F32) * (2.0 * math.pi / N)
    cs, sn = jnp.cos(ang), jnp.sin(ang)
    alt = (1 - 2 * (a & 1)).astype(F32)
    first = (a == 0)
    fs = jnp.where(first[:, None], alt[None, :], -sn)
    F = jnp.stack([cs, fs]).astype(BF16)
    return F, F[1].T


def _kf_kernel(fc_ref, fs_ref, hf_ref, hb_ref, o_ref, hs_s, hd_s, nyq_s):
    @pl.when(pl.program_id(2) == 0)
    def _():
        hf = hf_ref[...]
        rows = lax.broadcasted_iota(jnp.int32, hf.shape, 0)
        hb = jnp.where(rows == 0, 0.0, hb_ref[...])
        hs = hf + hb
        hs_s[...] = hs.astype(BF16)
        hd_s[...] = (hf - hb).astype(BF16)
        alt = (1 - 2 * (rows & 1)).astype(F32)
        nyq_s[...] = jnp.broadcast_to(jnp.sum(hs * alt, axis=0, keepdims=True), nyq_s.shape)

    kr = jnp.dot(fc_ref[...], hs_s[...], preferred_element_type=F32)
    ki = jnp.dot(fs_ref[...], hd_s[...], preferred_element_type=F32)
    bf = kr.shape[0]
    frow = lax.broadcasted_iota(jnp.int32, kr.shape, 0) + pl.program_id(2) * bf
    o_ref[0] = kr
    o_ref[1] = jnp.where(frow == 0, nyq_s[0:1, :], ki)


def _kf(F, hfilt, L):
    bf = min(512, L)
    cw = 512
    ncb = MIX // cw
    return pl.pallas_call(
        _kf_kernel, name="hy_filter_dft",
        out_shape=jax.ShapeDtypeStruct((2, 2, L, MIX), F32),
        grid=(2, ncb, L // bf),
        in_specs=[pl.BlockSpec((None, bf, L), lambda o, c, f: (0, f, 0)),
                  pl.BlockSpec((None, bf, L), lambda o, c, f: (1, f, 0)),
                  pl.BlockSpec((L, cw), lambda o, c, f: (0, o * ncb + c)),
                  pl.BlockSpec((L, cw), lambda o, c, f: (0, (2 + o) * ncb + c))],
        out_specs=pl.BlockSpec((None, 2, bf, cw), lambda o, c, f: (o, 0, f, c)),
        scratch_shapes=[pltpu.VMEM((L, cw), BF16)] * 2 + [pltpu.VMEM((8, cw), F32)],
        compiler_params=_cp(("arbitrary", "arbitrary", "arbitrary"), 48),
    )(F, F, hfilt, hfilt)


def _hyfwd_kernel(z_ref, fc_ref, fs_ref, kr_ref, ki_ref, o_ref):
    zb = z_ref[...]
    zr = jnp.dot(fc_ref[...], zb, preferred_element_type=F32)
    zi = jnp.dot(fs_ref[...], zb, preferred_element_type=F32)
    kr, ki = kr_ref[...], ki_ref[...]
    bf = zr.shape[0]
    frow = lax.broadcasted_iota(jnp.int32, zr.shape, 0) + pl.program_id(1) * bf
    dc = frow == 0
    n_circ = 2 * zb.shape[0]
    wf = jnp.where(dc, 1.0 / n_circ, 2.0 / n_circ)
    o_ref[0] = (jnp.where(dc, zr * kr, zr * kr - zi * ki) * wf).astype(o_ref.dtype)
    o_ref[1] = (jnp.where(dc, zi * ki, zr * ki + zi * kr) * wf).astype(o_ref.dtype)


def _hyfwd(z, zcol, F, KF, o, S, nseq):
    bf = min(512, S)
    return pl.pallas_call(
        _hyfwd_kernel, name="hy_fwd_dft",
        out_shape=jax.ShapeDtypeStruct((nseq, 2, S, MIX), BF16),
        grid=(nseq, S // bf),
        in_specs=[pl.BlockSpec((S, MIX), lambda s, f: (s, zcol)),
                  pl.BlockSpec((None, bf, S), lambda s, f: (0, f, 0)),
                  pl.BlockSpec((None, bf, S), lambda s, f: (1, f, 0)),
                  pl.BlockSpec((None, None, bf, MIX), lambda s, f: (o, 0, f, 0)),
                  pl.BlockSpec((None, None, bf, MIX), lambda s, f: (o, 1, f, 0))],
        out_specs=pl.BlockSpec((None, 2, bf, MIX), lambda s, f: (s, 0, f, 0)),
        compiler_params=_cp(("arbitrary", "arbitrary"), 48),
    )(z, F, F, KF, KF)


def _hyinv_kernel(gc_ref, gs_ref, pr_ref, pi_ref, z_ref, gate_ref, skip_ref, o_ref):
    y = (jnp.dot(gc_ref[...], pr_ref[...], preferred_element_type=F32)
         + jnp.dot(gs_ref[...], pi_ref[...], preferred_element_type=F32))
    y = y + z_ref[...].astype(F32) * skip_ref[...]
    o_ref[...] = (gate_ref[...].astype(F32) * y).astype(o_ref.dtype)


def _hyinv(F, FT, P, z, zcol, u3, gcol, hy_bias, l, o, S, nseq, out_dtype):
    bn = min(512, S)
    nb = S // bn
    return pl.pallas_call(
        _hyinv_kernel, name="hy_inv_dft",
        out_shape=jax.ShapeDtypeStruct((nseq * S, MIX), out_dtype),
        grid=(nseq, nb),
        in_specs=[pl.BlockSpec((None, bn, S), lambda s, n: (0, n, 0)),
                  pl.BlockSpec((bn, S), lambda s, n: (n, 0)),
                  pl.BlockSpec((None, None, S, MIX), lambda s, n: (s, 0, 0, 0)),
                  pl.BlockSpec((None, None, S, MIX), lambda s, n: (s, 1, 0, 0)),
                  pl.BlockSpec((bn, MIX), lambda s, n: (s * nb + n, zcol)),
                  pl.BlockSpec((bn, MIX), lambda s, n: (s * nb + n, gcol)),
                  pl.BlockSpec((None, None, 1, MIX), lambda s, n: (l, o, 0, 0))],
        out_specs=pl.BlockSpec((bn, MIX), lambda s, n: (s * nb + n, 0)),
        compiler_params=_cp(("arbitrary", "arbitrary"), 48),
    )(F, FT, P, P, z, u3, hy_bias)


def _hyena(p, F, FT, KF, hy_short_w, hy_short_b3, hy_bias4, l, S, nseq, row_off):
    u3 = _hyshort(p, hy_short_w, hy_short_b3, l, S, nseq, row_off)
    P = _hyfwd(u3, 0, F, KF, 0, S, nseq)
    z1 = _hyinv(F, FT, P, u3, 0, u3, 1, hy_bias4, l, 0, S, nseq, BF16)
    P = _hyfwd(z1, 0, F, KF, 1, S, nseq)
    return _hyinv(F, FT, P, z1, 0, u3, 2, hy_bias4, l, 1, S, nseq, BF16)


def _rope_tables():
    d_axis = DH // 2
    inv = ROPE_BASE ** (-jnp.arange(0, d_axis, 2, dtype=F32) / d_axis)
    t = jnp.arange(SEQ)
    pos = jnp.stack([t // GRID_W, t % GRID_W], axis=-1).astype(F32)
    lane = np.arange(DH)
    ang = pos[:, lane // d_axis] * inv[lane % (d_axis // 2)][None, :]
    sign = np.where((lane % d_axis) < d_axis // 2, -1.0, 1.0).astype(np.float32)
    return jnp.cos(ang), jnp.sin(ang) * sign[None, :]


def _toeplitz_kernel(r_ref, e_ref, o_ref):
    o_ref[...] = jnp.dot(r_ref[...], e_ref[...], precision=lax.Precision.HIGHEST, preferred_element_type=F32)


def _attn_bias(rpb_l):
    n_dr, n_dc = 2 * WIN_R - 1, 2 * WIN_C - 1
    d = lax.broadcasted_iota(jnp.int32, (128, GRID_W * GRID_W), 0)
    cw = lax.broadcasted_iota(jnp.int32, (128, GRID_W * GRID_W), 1)
    c, w = cw // GRID_W, cw % GRID_W
    c0 = jnp.clip(c - WIN_C // 2, 0, GRID_W - WIN_C)
    col_in = (w >= c0) & (w < c0 + WIN_C)
    sel = jnp.where(col_in, d == jnp.clip(w - c + (WIN_C - 1), 0, n_dc - 1), d == n_dc).astype(F32)
    tab = jnp.concatenate([rpb_l.reshape(HEADS * n_dr, n_dc).astype(F32),
                           jnp.full((HEADS * n_dr, 1), NEG, F32)], axis=1)
    tab = jnp.pad(tab, ((0, 128 - HEADS * n_dr), (0, 128 - n_dc - 1)))
    toe = pl.pallas_call(
        _toeplitz_kernel, name="na_bias_toeplitz",
        out_shape=jax.ShapeDtypeStruct((128, GRID_W * GRID_W), F32),
    )(tab, sel)
    toe = toe[:HEADS * n_dr].reshape(HEADS, n_dr, GRID_W, GRID_W)
    masked = jnp.full((HEADS, GRID_W, GRID_W), NEG, F32)
    nblk = GRID_H // ATT_RQ
    out = []
    for blk in (0, 1, nblk - 1):
        start0 = int(np.clip(ATT_RQ * blk - WIN_R // 2, 0, GRID_H - ATT_KR))
        rows = []
        for j in range(ATT_RQ):
            r = ATT_RQ * blk + j
            rs = int(np.clip(r - WIN_R // 2, 0, GRID_H - WIN_R))
            tiles = []
            for ir in range(ATT_KR):
                kr = start0 + ir
                tiles.append(toe[:, kr - r + (WIN_R - 1)] if rs <= kr < rs + WIN_R else masked)
            rows.append(jnp.concatenate(tiles, axis=-1))
        out.append(jnp.concatenate(rows, axis=1))
    return jnp.stack(out, axis=1)


def _rope(x, cos, sin_signed):
    lane = lax.broadcasted_iota(jnp.int32, x.shape, 1)
    partner = jnp.where((lane & 32) == 0, pltpu.roll(x, DH - 32, axis=1), pltpu.roll(x, 32, axis=1))
    return x * cos + partner * sin_signed


def _attn_kernel(q_ref, k_ref, v_ref, kc_ref, vc_ref, cos_ref, sin_ref, bias_ref, o_ref,
                 qr_s, qp_s, kr_s, v_s):
    scale = DH ** -0.5
    q = q_ref[...].astype(F32) * scale
    cos, sin = cos_ref[...], sin_ref[...]
    qp_s[...] = q.astype(BF16)
    qr_s[...] = _rope(q, cos, sin).astype(BF16)
    kr_s[...] = _rope(k_ref[...].astype(F32), cos, sin).astype(BF16)
    v_s[...] = v_ref[...].astype(BF16)
    kc = kc_ref[...].astype(BF16)
    vc = vc_ref[...].astype(BF16)
    QB, KW = ATT_RQ * GRID_W, ATT_KR * GRID_W
    nblk = GRID_H // ATT_RQ
    tb = (((1,), (1,)), ((), ()))

    def body(i, carry):
        q0 = pl.multiple_of(i * QB, QB)
        start0 = jnp.clip(ATT_RQ * i - WIN_R // 2, 0, GRID_H - ATT_KR)
        k0 = pl.multiple_of(start0 * GRID_W, GRID_W)
        var = jnp.where(i == 0, 0, jnp.where(i == nblk - 1, 2, 1))
        s_loc = lax.dot_general(qr_s[pl.ds(q0, QB), :], kr_s[pl.ds(k0, KW), :], tb,
                                preferred_element_type=F32) + bias_ref[var]
        s_ctx = lax.dot_general(qp_s[pl.ds(q0, QB), :], kc, tb, preferred_element_type=F32)
        m = jnp.maximum(jnp.max(s_loc, axis=-1, keepdims=True), jnp.max(s_ctx, axis=-1, keepdims=True))
        p_loc = jnp.exp(s_loc - m)
        p_ctx = jnp.exp(s_ctx - m)
        den = jnp.sum(p_loc, axis=-1, keepdims=True) + jnp.sum(p_ctx, axis=-1, keepdims=True)
        o = (jnp.dot(p_loc.astype(BF16), v_s[pl.ds(k0, KW), :], preferred_element_type=F32)
             + jnp.dot(p_ctx.astype(BF16), vc, preferred_element_type=F32))
        o_ref[pl.ds(q0, QB), :] = (o / den).astype(o_ref.dtype)
        return carry

    lax.fori_loop(0, nblk, body, 0, unroll=2)


def _attn(p, pc, c_rowblk, c_kcol, c_vcol, cos, sin, bias):
    qc, kc_, vc_ = 3 * MIX // DH, 4 * MIX // DH, 5 * MIX // DH
    lat = lambda c: pl.BlockSpec((SEQ, DH), lambda h, b: (b, c + h))
    ctx = lambda c: pl.BlockSpec((CTX, DH), lambda h, b: (c_rowblk + b, c + h))
    tab = pl.BlockSpec((SEQ, DH), lambda h, b: (0, 0))
    QB, KW = ATT_RQ * GRID_W, ATT_KR * GRID_W
    return pl.pallas_call(
        _attn_kernel, name="na_attention",
        out_shape=jax.ShapeDtypeStruct((N_LAT, MIX), BF16),
        grid=(HEADS, BATCH),
        in_specs=[lat(qc), lat(kc_), lat(vc_), ctx(c_kcol), ctx(c_vcol), tab, tab,
                  pl.BlockSpec((None, 3, QB, KW), lambda h, b: (h, 0, 0, 0))],
        out_specs=pl.BlockSpec((SEQ, DH), lambda h, b: (b, h)),
        scratch_shapes=[pltpu.VMEM((SEQ, DH), BF16)] * 4,
        compiler_params=_cp(("arbitrary", "arbitrary"), 40),
    )(p, p, p, pc, pc, cos, sin, bias)


def _ctxattn_kernel(q_ref, k_ref, v_ref, o_ref):
    scale = DH ** -0.5
    q = (q_ref[...].astype(F32) * scale).astype(BF16)
    s = lax.dot_general(q, k_ref[...].astype(BF16), (((1,), (1,)), ((), ())), preferred_element_type=F32)
    m = jnp.max(s, axis=-1, keepdims=True)
    e = jnp.exp(s - m)
    den = jnp.sum(e, axis=-1, keepdims=True)
    o = jnp.dot(e.astype(BF16), v_ref[...].astype(BF16), preferred_element_type=F32)
    o_ref[...] = (o / den).astype(o_ref.dtype)


def _ctxattn(p):
    rb = N_LAT // CTX
    qc, kc_, vc_ = 3 * MIX // DH, 4 * MIX // DH, 5 * MIX // DH
    blk = lambda c: pl.BlockSpec((CTX, DH), lambda b, h: (rb + b, c + h))
    return pl.pallas_call(
        _ctxattn_kernel, name="ctx_attention",
        out_shape=jax.ShapeDtypeStruct((N_CTX, MIX), BF16),
        grid=(BATCH, HEADS),
        in_specs=[blk(qc), blk(kc_), blk(vc_)],
        out_specs=pl.BlockSpec((CTX, DH), lambda b, h: (b, h)),
        compiler_params=_cp(("arbitrary", "arbitrary"), 32),
    )(p, p, p)


def _pad2(a, r, c):
    return jnp.pad(a, ((0, r - a.shape[0]), (0, c - a.shape[1])))


def kernel(x, c, ctx, c_ctx, ada_w, ada_b, norm_g, ffn1_in, ffn1_out, ffn2_in, ffn2_out, w_in, pool_w, pool_scale, conv_dw_w, conv_dw_b, conv_ln_g, conv_ln_b, na_rpb, hy_short_w, hy_short_b, hy_w1, hy_b1, hy_freq1, hy_w2, hy_b2, hy_freq2, hy_w3, hy_decay, hy_bias, w_branch, w_out):
    X = (x.reshape(N_LAT, D), ctx.reshape(N_CTX, D))
    cc = jnp.concatenate([c, c_ctx[None], jnp.zeros((8 - BATCH - 1, D), F32)], axis=0)
    g3 = norm_g.reshape(DEPTH * 6, 1, D)
    ada_b3 = ada_b.reshape(DEPTH, 1, N_ADA * D)
    vec3 = lambda a: a.reshape(DEPTH, 1, a.shape[-1])
    pool_scale3, conv_dw_b3, conv_ln_g3, conv_ln_b3 = map(vec3, (pool_scale, conv_dw_b, conv_ln_g, conv_ln_b))
    hy_short_b3, hy_decay3 = vec3(hy_short_b), vec3(hy_decay)
    hy_bias4 = hy_bias.reshape(DEPTH, 2, 1, MIX)

    cos, sin = _rope_tables()
    bands = {SEQ: _pool_bands(SEQ), CTX: _pool_bands(CTX)}
    dft = {SEQ: _dft_mats(SEQ), CTX: _dft_mats(CTX)}
    feats = {SEQ: _hy_feats(SEQ), CTX: _hy_feats(CTX)}

    m3 = [_ada(cc, ada_w, ada_b3, l).reshape(8, 1, N_ADA * D) for l in range(DEPTH)]

    rows = N_ALL
    h = _normmod(X[0], X[1], g3, m3[0], 0, 0, 0)
    for l in range(DEPTH):
        last = l == DEPTH - 1
        m = m3[l]

        hid = _ffn_up(h, ffn1_in, l, rows)
        y = _matmul(hid, ffn1_out, l, rows, D, BF16, name="ffn_down", ni=8, bn=512)
        X, h = _resid(X, y, g3, m, l, 1, 2, 0.5, rows, nxt=(g3, m, l, 2, 3))

        if not last:
            p = _matmul(h, w_in, l, rows, PROJ_W, P_DT, name="proj_in", ni=8, bn=1024)
            pc, c_rowblk, c_kcol, c_vcol = p, N_LAT // CTX, 4 * MIX // DH, 5 * MIX // DH
        else:
            p = _matmul(h, w_in, l, N_LAT, PROJ_W, P_DT, name="proj_in", ni=8, bn=1024)
            pc = _matmul(h, w_in, l, N_CTX, 2 * MIX, P_DT, name="proj_ctx_kv", ni=2, bn=1024,
                         row_off=N_LAT, col_off=4 * MIX)
            c_rowblk, c_kcol, c_vcol = 0, 0, MIX // DH

        fpar = (_pad2(hy_w1[l], 128, 128), _pad2(hy_b1[l][None], 1, 128), _pad2(hy_freq1[l][None], 1, 128),
                _pad2(hy_w2[l], 128, 128), _pad2(hy_b2[l][None], 1, 128), _pad2(hy_freq2[l][None], 1, 128),
                _pad2(hy_w3[l], 128, 4 * MIX))

        def mixers(S, nseq, row_off):
            F, G = dft[S]
            KF = _kf(F, _hyfilt(feats[S], *fpar, hy_decay3, l, S), S)
            y0 = _pool(p, bands[S], pool_w, pool_scale3, l, S, nseq, row_off)
            y1 = _conv(p, conv_dw_w, conv_dw_b3, conv_ln_g3, conv_ln_b3, l, S, nseq, row_off)
            y3 = _hyena(p, F, G, KF, hy_short_w, hy_short_b3, hy_bias4, l, S, nseq, row_off)
            return y0, y1, y3

        y0, y1, y3 = mixers(SEQ, BATCH, 0)
        y2 = _attn(p, pc, c_rowblk, c_kcol, c_vcol, cos, sin, _attn_bias(na_rpb[l]))
        if not last:
            c0, c1, c3 = mixers(CTX, BATCH, N_LAT)
            ys_ctx = [c0, c1, _ctxattn(p), c3]
            mrows = rows
        else:
            ys_ctx = None
            mrows = N_LAT
        merged = _branch([y0, y1, y2, y3], ys_ctx, p, w_branch, l, mrows)
        y = _matmul(merged, w_out, l, mrows, D, BF16, name="proj_out", ni=8, bn=1024)
        X, h = _resid(X, y, g3, m, l, 3, 5, 1.0, mrows, nxt=(g3, m, l, 4, 6))
        rows = mrows

        hid = _ffn_up(h, ffn2_in, l, rows)
        y = _matmul(hid, ffn2_out, l, rows, D, BF16, name="ffn_down", ni=8, bn=512)
        if not last:
            X, h = _resid(X, y, g3, m, l, 5, 8, 0.5, rows, nxt=(g3, m3[l + 1], l + 1, 0, 0))
        else:
            X = _resid(X, y, g3, m, l, 5, 8, 0.5, rows)
    return X.reshape(BATCH, SEQ, D)
```

```python
import functools
import math

import numpy as np
import jax
import jax.numpy as jnp
from jax import lax
from jax.experimental import pallas as pl
from jax.experimental.pallas import tpu as pltpu

F32 = jnp.float32
BF16 = jnp.bfloat16

D = 4096
BATCH = 4
SEQ = 2048
DEPTH = 2
GRID_W = 64
GRID_H = SEQ // GRID_W
CTX = 256
D_FF = 6144
MIX = D // 4
N_BRANCH = 4
POOL_WINDOWS = (2, 4, 8, 16)
POOL_G = MIX // len(POOL_WINDOWS)
CONV_K = 31
DH = 128
HEADS = MIX // DH
WIN_R = 8
WIN_C = 16
ROPE_BASE = 10000.0
HY_EMB = 33
HY_HID = 64
N_ADA = 9
EPS = 1e-6
PROJ_W = 9 * MIX + N_BRANCH * D
G_OFF = 9 * MIX

N_LAT = BATCH * SEQ
N_CTX = BATCH * CTX
N_ALL = N_LAT + N_CTX

P_DT = BF16
NEG = -1e30

ATT_RQ = 4
ATT_KR = 12
MIB = 1024 * 1024
STREAM_VMEM_MIB = 60


def _cp(sem, vmem_mib):
    return pltpu.CompilerParams(dimension_semantics=sem, vmem_limit_bytes=vmem_mib * MIB)


def _modrow(i, bm):
    return jnp.minimum((i * bm) // SEQ, BATCH)


def _silu(v):
    return v * jax.nn.sigmoid(v)


def _ada_kernel(c_ref, w_ref, b_ref, o_ref):
    s = _silu(c_ref[...]).astype(BF16)
    o_ref[...] = jnp.dot(s, w_ref[...].astype(BF16), preferred_element_type=F32) + b_ref[...]


def _ada(cc, ada_w, ada_b3, l):
    bn = 512
    return pl.pallas_call(
        _ada_kernel, name="ada",
        out_shape=jax.ShapeDtypeStruct((8, N_ADA * D), F32),
        grid=(N_ADA * D // bn,),
        in_specs=[pl.BlockSpec((8, D), lambda j: (0, 0)),
                  pl.BlockSpec((None, D, bn), lambda j: (l, 0, j)),
                  pl.BlockSpec((None, 1, bn), lambda j: (l, 0, j))],
        out_specs=pl.BlockSpec((8, bn), lambda j: (0, j)),
        compiler_params=_cp(("arbitrary",), 40),
    )(cc, ada_w, ada_b3)


def _rms(v, g):
    return v * lax.rsqrt(jnp.mean(v * v, axis=-1, keepdims=True) + EPS) * g


def _stream_rows(lat_ref, ctx_ref):
    bm = lat_ref.shape[0]
    return jnp.where(pl.program_id(0) * bm < N_LAT, lat_ref[...], ctx_ref[...])


def _stream_specs(bm):
    nl = N_LAT // bm
    return [pl.BlockSpec((bm, D), lambda i: (jnp.minimum(i, nl - 1), 0)),
            pl.BlockSpec((bm, D), lambda i: (jnp.maximum(i - nl, 0), 0))]


def _normmod_kernel(x_ref, c_ref, g_ref, sh_ref, sc_ref, h_ref):
    y = _rms(_stream_rows(x_ref, c_ref), g_ref[...])
    h_ref[...] = (y * (1.0 + sc_ref[...]) + sh_ref[...]).astype(h_ref.dtype)


def _mod_spec(bm, k):
    return pl.BlockSpec((None, 1, D), lambda i: (_modrow(i, bm), 0, k))


def _g_spec(l, k):
    return pl.BlockSpec((None, 1, D), lambda i: (l * 6 + k, 0, 0))


def _normmod(x_lat, x_ctx, g3, m3, l, gk, mk):
    bm = 256
    return pl.pallas_call(
        _normmod_kernel, name="normmod",
        out_shape=jax.ShapeDtypeStruct((N_ALL, D), BF16),
        grid=(N_ALL // bm,),
        in_specs=_stream_specs(bm) + [_g_spec(l, gk), _mod_spec(bm, mk), _mod_spec(bm, mk + 1)],
        out_specs=pl.BlockSpec((bm, D), lambda i: (i, 0)),
        compiler_params=_cp(("arbitrary",), 32),
    )(x_lat, x_ctx, g3, m3, m3)


def _resid_kernel(*refs, coef, with_next, split_x):
    if split_x:
        x = _stream_rows(refs[0], refs[1])
        y_ref, gp_ref, gate_ref, *rest = refs[2:]
    else:
        x = refs[0][...]
        y_ref, gp_ref, gate_ref, *rest = refs[1:]
    xn = x + coef * gate_ref[...] * _rms(y_ref[...].astype(F32), gp_ref[...])
    if with_next:
        gn_ref, sh_ref, sc_ref, xo_ref, h_ref = rest
        xo_ref[...] = xn
        h_ref[...] = (_rms(xn, gn_ref[...]) * (1.0 + sc_ref[...]) + sh_ref[...]).astype(h_ref.dtype)
    else:
        (xo_ref,) = rest
        xo_ref[...] = xn


def _resid(x, y, g3, m3, l, gk_post, mk_gate, coef, rows, nxt=None):
    bm = 256
    split_x = isinstance(x, tuple)
    x_specs = _stream_specs(bm) if split_x else [pl.BlockSpec((bm, D), lambda i: (i, 0))]
    in_specs = x_specs + [pl.BlockSpec((bm, D), lambda i: (i, 0)), _g_spec(l, gk_post), _mod_spec(bm, mk_gate)]
    args = (list(x) if split_x else [x]) + [y, g3, m3]
    out_shape = [jax.ShapeDtypeStruct((rows, D), F32)]
    out_specs = [pl.BlockSpec((bm, D), lambda i: (i, 0))]
    if nxt is not None:
        g3n, m3n, ln, gkn, mkn = nxt
        in_specs += [_g_spec(ln, gkn), _mod_spec(bm, mkn), _mod_spec(bm, mkn + 1)]
        args += [g3n, m3n, m3n]
        out_shape.append(jax.ShapeDtypeStruct((rows, D), BF16))
        out_specs.append(pl.BlockSpec((bm, D), lambda i: (i, 0)))
    res = pl.pallas_call(
        functools.partial(_resid_kernel, coef=coef, with_next=nxt is not None, split_x=split_x), name="resid",
        out_shape=out_shape, grid=(rows // bm,), in_specs=in_specs, out_specs=out_specs,
        compiler_params=_cp(("arbitrary",), 48),
    )(*args)
    return res if nxt is not None else res[0]


def _stream_kernel(*refs, n_x, n_w, n_e, n_chunks, combine):
    xs = refs[:n_x]
    ws = refs[n_x:n_x + n_w]
    es = refs[n_x + n_w:n_x + n_w + n_e]
    o_ref = refs[n_x + n_w + n_e]
    slots = refs[n_x + n_w + n_e + 1:]
    slot_a, slot_b = slots[:n_w], slots[n_w:]
    j, i = pl.program_id(0), pl.program_id(1)

    def stage(dst):
        for w_ref, d in zip(ws, dst):
            ck = w_ref.shape[-2]
            r0 = pl.multiple_of(jnp.minimum(i, n_chunks - 1) * ck, ck)
            if len(w_ref.shape) == 3:
                d[:, pl.ds(r0, ck), :] = w_ref[...].astype(BF16)
            else:
                d[pl.ds(r0, ck), :] = w_ref[...].astype(BF16)

    def compute(src):
        o_ref[...] = combine(xs, src, es).astype(o_ref.dtype)

    @pl.when(j == 0)
    def _():
        stage(slot_a)
        o_ref[...] = jnp.zeros(o_ref.shape, o_ref.dtype)

    @pl.when(j % 2 == 1)
    def _():
        stage(slot_b)
        compute(slot_a)

    @pl.when((j > 0) & (j % 2 == 0))
    def _():
        stage(slot_a)
        compute(slot_b)


def _comb_plain(xs, ws, es):
    return jnp.dot(xs[0][...], ws[0][...], preferred_element_type=F32)


def _comb_swiglu(xs, ws, es):
    x = xs[0][...]
    a = jnp.dot(x, ws[0][...], preferred_element_type=F32)
    b = jnp.dot(x, ws[1][...], preferred_element_type=F32)
    return _silu(a) * b


def _comb_branch(xs, ws, es, n_lat_tiles=None):
    acc = None
    for br in range(N_BRANCH):
        y = xs[br][...]
        if n_lat_tiles is not None:
            y = jnp.where(pl.program_id(1) < n_lat_tiles, y, xs[N_BRANCH + br][...])
        t = jax.nn.sigmoid(es[br][...].astype(F32)) * jnp.dot(y, ws[0][br], preferred_element_type=F32)
        acc = t if acc is None else acc + t
    return acc


def _row_idx(j, i):
    return jnp.where(j == 0, 0, i)


def _stream_call(name, combine, xs, x_k, ws, w_specs, w_slots, es, e_specs, rows, n_out, out_dtype,
                 bm, bn, ro, vmem, n_chunks=None, x_specs=None):
    nj, ni = n_out // bn, rows // bm
    if x_specs is None:
        x_specs = [pl.BlockSpec((bm, x_k), lambda j, i: (_row_idx(j, i) + ro, 0))] * len(xs)
    return pl.pallas_call(
        functools.partial(_stream_kernel, n_x=len(xs), n_w=len(ws), n_e=len(es),
                          n_chunks=ni if n_chunks is None else n_chunks, combine=combine), name=name,
        out_shape=jax.ShapeDtypeStruct((rows, n_out), out_dtype),
        grid=(nj + 1, ni),
        in_specs=x_specs + w_specs + e_specs,
        out_specs=pl.BlockSpec((bm, bn), lambda j, i: (_row_idx(j, i), jnp.maximum(j - 1, 0))),
        scratch_shapes=w_slots + w_slots,
        compiler_params=_cp(("arbitrary", "arbitrary"), vmem),
    )(*xs, *ws, *es)


def _matmul(x, w, l, rows, n_out, out_dtype, *, name, ni, bn, row_off=0, col_off=0, vmem=STREAM_VMEM_MIB):
    K = x.shape[1]
    bm, ck, nj = rows // ni, K // ni, n_out // bn
    co = col_off // bn
    w_specs = [pl.BlockSpec((None, ck, bn), lambda j, i: (l, i, jnp.minimum(j, nj - 1) + co))]
    return _stream_call(name, _comb_plain, [x], K, [w], w_specs, [pltpu.VMEM((K, bn), BF16)], [], [],
                        rows, n_out, out_dtype, bm, bn, row_off // bm, vmem)


def _ffn_up(h, w_up, l, rows):
    ni, bn = 8, 512
    bm, ck, nj = rows // ni, D // ni, D_FF // bn
    w_specs = [pl.BlockSpec((None, ck, bn), lambda j, i: (l, i, jnp.minimum(j, nj - 1))),
               pl.BlockSpec((None, ck, bn), lambda j, i: (l, i, jnp.minimum(j, nj - 1) + nj))]
    return _stream_call("ffn_up", _comb_swiglu, [h], D, [w_up, w_up], w_specs, [pltpu.VMEM((D, bn), BF16)] * 2,
                        [], [], rows, D_FF, BF16, bm, bn, 0, STREAM_VMEM_MIB)


def _branch(ys, ys_ctx, p, w_branch, l, rows):
    n_chunks, bn, bm = 8, 512, N_LAT // 8
    ck, nj = MIX // n_chunks, D // bn
    w_specs = [pl.BlockSpec((None, N_BRANCH, ck, bn),
                            lambda j, i: (l, 0, jnp.minimum(i, n_chunks - 1), jnp.minimum(j, nj - 1)))]
    e_specs = [pl.BlockSpec((bm, bn), functools.partial(
        lambda j, i, br: (_row_idx(j, i), (G_OFF + br * D) // bn + jnp.maximum(j - 1, 0)), br=br))
        for br in range(N_BRANCH)]
    xs, x_specs, n_lat_tiles = list(ys), None, None
    if ys_ctx is not None:
        n_lat_tiles = N_LAT // bm
        xs = xs + list(ys_ctx)
        x_specs = ([pl.BlockSpec((bm, MIX), lambda j, i: (jnp.minimum(_row_idx(j, i), n_lat_tiles - 1), 0))] * N_BRANCH
                   + [pl.BlockSpec((bm, MIX), lambda j, i: (0, 0))] * N_BRANCH)
    return _stream_call("branch_merge", functools.partial(_comb_branch, n_lat_tiles=n_lat_tiles), xs, MIX,
                        [w_branch], w_specs, [pltpu.VMEM((N_BRANCH, MIX, bn), BF16)], [p] * N_BRANCH, e_specs,
                        rows, D, BF16, bm, bn, 0, STREAM_VMEM_MIB, n_chunks=n_chunks, x_specs=x_specs)


def _pool_bands(S):
    t = jnp.arange(S, dtype=jnp.int32)[:, None]
    m = jnp.arange(S, dtype=jnp.int32)[None, :]
    bands = []
    for w in POOL_WINDOWS:
        lo = jnp.clip(t - w // 2, 0, S)
        hi = jnp.clip(t + w // 2, 0, S)
        inside = (m >= lo) & (m < hi)
        band = jnp.where(inside, 1.0 / (hi - lo).astype(F32), 0.0) - (m == t).astype(F32)
        bands.append(band)
    return jnp.stack(bands).astype(BF16)


POOL_BLK = 256


def _pool_kernel(band_ref, p_ref, w_ref, sc_ref, o_ref):
    S = p_ref.shape[0]
    nb = S // POOL_BLK
    w = w_ref[...].astype(BF16)
    for r in range(nb):
        r0, r1 = r * POOL_BLK, (r + 1) * POOL_BLK
        k0, k1 = max(r - 1, 0) * POOL_BLK, min(r + 2, nb) * POOL_BLK
        y = jnp.dot(band_ref[r0:r1, k0:k1], p_ref[k0:k1, :].astype(BF16), preferred_element_type=F32)
        z = jnp.dot(y.astype(BF16), w, preferred_element_type=F32)
        o_ref[r0:r1, :] = (z * sc_ref[...]).astype(o_ref.dtype)


def _pool(p, bands, pool_w, pool_scale3, l, S, nseq, row_off):
    ro = row_off // S
    ng = len(POOL_WINDOWS)
    return pl.pallas_call(
        _pool_kernel, name="pool",
        out_shape=jax.ShapeDtypeStruct((nseq * S, MIX), BF16),
        grid=(ng, nseq),
        in_specs=[pl.BlockSpec((None, S, S), lambda g, s: (g, 0, 0)),
                  pl.BlockSpec((S, POOL_G), lambda g, s: (s + ro, g)),
                  pl.BlockSpec((None, None, POOL_G, POOL_G), lambda g, s: (l, g, 0, 0)),
                  pl.BlockSpec((None, 1, POOL_G), lambda g, s: (l, 0, g))],
        out_specs=pl.BlockSpec((S, POOL_G), lambda g, s: (s, g)),
        compiler_params=_cp(("arbitrary", "arbitrary"), 40),
    )(bands, p, pool_w, pool_scale3)


CONV_PAD = 16
CONV_CH = 32


def _conv_kernel(a_ref, g_ref, w_ref, b_ref, lg_ref, lb_ref, o_ref, u_ref, *, S):
    C = a_ref.shape[1]
    zeros = jnp.zeros((CONV_PAD, C), F32)
    u_ref[0:CONV_PAD, :] = zeros
    u_ref[S + CONV_PAD:S + 2 * CONV_PAD, :] = zeros
    SC = 128

    def stage(i, carry):
        r0 = pl.multiple_of(i * SC, SC)
        a = a_ref[pl.ds(r0, SC), :].astype(F32)
        g = g_ref[pl.ds(r0, SC), :].astype(F32)
        u_ref[pl.ds(r0 + CONV_PAD, SC), :] = a * jax.nn.sigmoid(g)
        return carry

    lax.fori_loop(0, S // SC, stage, 0)

    n = CONV_CH + 2 * CONV_PAD
    off = CONV_PAD - CONV_K // 2

    def body(i, carry):
        r0 = pl.multiple_of(i * CONV_CH, CONV_CH)
        win = u_ref[pl.ds(r0, n), :]
        acc = jnp.broadcast_to(b_ref[...], (CONV_CH, C))
        for sub in range(8):
            wsub = win if sub == 0 else pltpu.roll(win, n - sub, axis=0)
            for al in range(n // 8):
                k = 8 * al + sub - off
                if 0 <= k < CONV_K:
                    acc = acc + w_ref[k:k + 1, :] * wsub[8 * al:8 * al + CONV_CH]
        mu = jnp.mean(acc, axis=-1, keepdims=True)
        xc = acc - mu
        var = jnp.mean(xc * xc, axis=-1, keepdims=True)
        y = xc * lax.rsqrt(var + EPS) * lg_ref[...] + lb_ref[...]
        o_ref[pl.ds(r0, CONV_CH), :] = _silu(y).astype(o_ref.dtype)
        return carry

    lax.fori_loop(0, S // CONV_CH, body, 0)


def _conv(p, dw_w, dw_b3, ln_g3, ln_b3, l, S, nseq, row_off):
    ro = row_off // S
    vec = lambda: pl.BlockSpec((None, 1, MIX), lambda s: (l, 0, 0))
    return pl.pallas_call(
        functools.partial(_conv_kernel, S=S), name="conv_module",
        out_shape=jax.ShapeDtypeStruct((nseq * S, MIX), BF16),
        grid=(nseq,),
        in_specs=[pl.BlockSpec((S, MIX), lambda s: (s + ro, 1)),
                  pl.BlockSpec((S, MIX), lambda s: (s + ro, 2)),
                  pl.BlockSpec((None, CONV_K, MIX), lambda s: (l, 0, 0)),
                  vec(), vec(), vec()],
        out_specs=pl.BlockSpec((S, MIX), lambda s: (s, 0)),
        scratch_shapes=[pltpu.VMEM((S + 2 * CONV_PAD, MIX), F32)],
        compiler_params=_cp(("arbitrary",), 48),
    )(p, p, dw_w, dw_b3, ln_g3, ln_b3)


HY_PAD = 8
HY_CH = 64


def _hyshort_kernel(p_ref, w_ref, b_ref, o_ref, u_ref, *, S):
    C = p_ref.shape[1]
    zeros = jnp.zeros((HY_PAD, C), F32)
    u_ref[0:HY_PAD, :] = zeros
    u_ref[S + HY_PAD:S + 2 * HY_PAD, :] = zeros
    SC = 128

    def stage(i, carry):
        r0 = pl.multiple_of(i * SC, SC)
        u_ref[pl.ds(r0 + HY_PAD, SC), :] = p_ref[pl.ds(r0, SC), :].astype(F32)
        return carry

    lax.fori_loop(0, S // SC, stage, 0)
    n = HY_CH + 2 * HY_PAD

    def body(i, carry):
        r0 = pl.multiple_of(i * HY_CH, HY_CH)
        win = u_ref[pl.ds(r0, n), :]
        prev = pltpu.roll(win, n - (HY_PAD - 1), axis=0)[0:HY_CH]
        mid = win[HY_PAD:HY_PAD + HY_CH]
        nxt = pltpu.roll(win, n - 1, axis=0)[HY_PAD:HY_PAD + HY_CH]
        o_ref[pl.ds(r0, HY_CH), :] = (w_ref[0:1, :] * prev + w_ref[1:2, :] * mid + w_ref[2:3, :] * nxt
                                       + b_ref[...]).astype(o_ref.dtype)
        return carry

    lax.fori_loop(0, S // HY_CH, body, 0)


def _hyshort(p, sw, sb3, l, S, nseq, row_off):
    ro = row_off // S
    cb = (6 * MIX) // MIX
    return pl.pallas_call(
        functools.partial(_hyshort_kernel, S=S), name="hy_short",
        out_shape=jax.ShapeDtypeStruct((nseq * S, 3 * MIX), BF16),
        grid=(nseq, 3),
        in_specs=[pl.BlockSpec((S, MIX), lambda s, c: (s + ro, cb + c)),
                  pl.BlockSpec((None, 3, MIX), lambda s, c: (l, 0, c)),
                  pl.BlockSpec((None, 1, MIX), lambda s, c: (l, 0, c))],
        out_specs=pl.BlockSpec((S, MIX), lambda s, c: (s, c)),
        scratch_shapes=[pltpu.VMEM((S + 2 * HY_PAD, MIX), F32)],
        compiler_params=_cp(("arbitrary", "arbitrary"), 48),
    )(p, sw, sb3)


def _hy_feats(L):
    t = jnp.linspace(0.0, 1.0, L, dtype=F32)[:, None]
    omega = (2.0 * math.pi / L) * jnp.arange(L, dtype=F32)[:, None]
    bands = (HY_EMB - 1) // 2
    freqs = jnp.linspace(1e-4, bands - 1, bands, dtype=F32)[None, :]
    z = jnp.concatenate([t, jnp.cos(freqs * omega), -jnp.sin(freqs * omega)], axis=-1)
    return jnp.pad(z, ((0, 0), (0, 128 - HY_EMB)))


HYF_CH = 256


def _hyfilt_kernel(z_ref, w1_ref, b1_ref, f1_ref, w2_ref, b2_ref, f2_ref, w3_ref, dec_ref, o_ref, *, L):
    hp = lax.Precision.HIGHEST
    h = jnp.sin(f1_ref[...] * (jnp.dot(z_ref[...], w1_ref[...], precision=hp, preferred_element_type=F32)
                               + b1_ref[...]))
    h = jnp.sin(f2_ref[...] * (jnp.dot(h, w2_ref[...], precision=hp, preferred_element_type=F32) + b2_ref[...]))
    o = jnp.dot(h, w3_ref[...], precision=hp, preferred_element_type=F32)
    r0 = pl.program_id(0) * HYF_CH
    t = (lax.broadcasted_iota(jnp.int32, o.shape, 0) + r0).astype(F32) * (1.0 / (L - 1))
    o_ref[...] = o * jnp.exp(-t * jnp.abs(dec_ref[...]))


def _hyfilt(zf, w1p, b1p, f1p, w2p, b2p, f2p, w3p, dec3, l, L):
    nfc = 4 * MIX
    small = lambda shp: pl.BlockSpec(shp, lambda i: (0,) * len(shp))
    return pl.pallas_call(
        functools.partial(_hyfilt_kernel, L=L), name="hy_filter",
        out_shape=jax.ShapeDtypeStruct((L, nfc), F32),
        grid=(L // HYF_CH,),
        in_specs=[pl.BlockSpec((HYF_CH, 128), lambda i: (i, 0)), small((128, 128)), small((1, 128)), small((1, 128)),
                  small((128, 128)), small((1, 128)), small((1, 128)), small((128, nfc)),
                  pl.BlockSpec((None, 1, nfc), lambda i: (l, 0, 0))],
        out_specs=pl.BlockSpec((HYF_CH, nfc), lambda i: (i, 0)),
        compiler_params=_cp(("arbitrary",), 40),
    )(zf, w1p, b1p, f1p, w2p, b2p, f2p, w3p, dec3)


DFT_SPLIT = 64


def _dft_mats(L):
    N = 2 * L
    a = jnp.arange(L, dtype=jnp.int32)

    def trig(mult, n):
        idx = ((mult * jnp.arange(n, dtype=jnp.int32))[:, None] * a[None, :]) & (N - 1)
        ang = idx.astype(F32) * (2.0 * math.pi / N)
        return jnp.cos(ang), jnp.sin(ang)

    ch, sh = trig(DFT_SPLIT, L // DFT_SPLIT)
    cl, sl = trig(1, DFT_SPLIT)
    cs = (ch[:, None, :] * cl[None] - sh[:, None, :] * sl[None]).reshape(L, L)
    sn = (sh[:, None, :] * cl[None] + ch[:, None, :] * sl[None]).reshape(L, L)
    alt = (1 - 2 * (a & 1)).astype(F32)
    first = (a == 0)
    fs = jnp.where(first[:, None], alt[None, :], -sn)
    F = jnp.stack([cs, fs]).astype(BF16)
    return F, F[1].T


def _kf_kernel(fc_ref, fs_ref, hf_ref, hb_ref, o_ref, hs_s, hd_s, nyq_s):
    @pl.when(pl.program_id(2) == 0)
    def _():
        hf = hf_ref[...]
        rows = lax.broadcasted_iota(jnp.int32, hf.shape, 0)
        hb = jnp.where(rows == 0, 0.0, hb_ref[...])
        hs = hf + hb
        hs_s[...] = hs.astype(BF16)
        hd_s[...] = (hf - hb).astype(BF16)
        alt = (1 - 2 * (rows & 1)).astype(F32)
        nyq_s[...] = jnp.broadcast_to(jnp.sum(hs * alt, axis=0, keepdims=True), nyq_s.shape)

    kr = jnp.dot(fc_ref[...], hs_s[...], preferred_element_type=F32)
    ki = jnp.dot(fs_ref[...], hd_s[...], preferred_element_type=F32)
    bf = kr.shape[0]
    frow = lax.broadcasted_iota(jnp.int32, kr.shape, 0) + pl.program_id(2) * bf
    o_ref[0] = kr
    o_ref[1] = jnp.where(frow == 0, nyq_s[0:1, :], ki)


def _kf(F, hfilt, L):
    bf = min(512, L)
    cw = 512
    ncb = MIX // cw
    return pl.pallas_call(
        _kf_kernel, name="hy_filter_dft",
        out_shape=jax.ShapeDtypeStruct((2, 2, L, MIX), F32),
        grid=(2, ncb, L // bf),
        in_specs=[pl.BlockSpec((None, bf, L), lambda o, c, f: (0, f, 0)),
                  pl.BlockSpec((None, bf, L), lambda o, c, f: (1, f, 0)),
                  pl.BlockSpec((L, cw), lambda o, c, f: (0, o * ncb + c)),
                  pl.BlockSpec((L, cw), lambda o, c, f: (0, (2 + o) * ncb + c))],
        out_specs=pl.BlockSpec((None, 2, bf, cw), lambda o, c, f: (o, 0, f, c)),
        scratch_shapes=[pltpu.VMEM((L, cw), BF16)] * 2 + [pltpu.VMEM((8, cw), F32)],
        compiler_params=_cp(("arbitrary", "arbitrary", "arbitrary"), 48),
    )(F, F, hfilt, hfilt)


def _hyfwd_kernel(z_ref, fc_ref, fs_ref, kr_ref, ki_ref, o_ref):
    zb = z_ref[...]
    zr = jnp.dot(fc_ref[...], zb, preferred_element_type=F32)
    zi = jnp.dot(fs_ref[...], zb, preferred_element_type=F32)
    kr, ki = kr_ref[...], ki_ref[...]
    bf = zr.shape[0]
    frow = lax.broadcasted_iota(jnp.int32, zr.shape, 0) + pl.program_id(1) * bf
    dc = frow == 0
    n_circ = 2 * zb.shape[0]
    wf = jnp.where(dc, 1.0 / n_circ, 2.0 / n_circ)
    o_ref[0] = (jnp.where(dc, zr * kr, zr * kr - zi * ki) * wf).astype(o_ref.dtype)
    o_ref[1] = (jnp.where(dc, zi * ki, zr * ki + zi * kr) * wf).astype(o_ref.dtype)


def _hyfwd(z, zcol, F, KF, o, S, nseq):
    bf = min(512, S)
    return pl.pallas_call(
        _hyfwd_kernel, name="hy_fwd_dft",
        out_shape=jax.ShapeDtypeStruct((nseq, 2, S, MIX), BF16),
        grid=(nseq, S // bf),
        in_specs=[pl.BlockSpec((S, MIX), lambda s, f: (s, zcol)),
                  pl.BlockSpec((None, bf, S), lambda s, f: (0, f, 0)),
                  pl.BlockSpec((None, bf, S), lambda s, f: (1, f, 0)),
                  pl.BlockSpec((None, None, bf, MIX), lambda s, f: (o, 0, f, 0)),
                  pl.BlockSpec((None, None, bf, MIX), lambda s, f: (o, 1, f, 0))],
        out_specs=pl.BlockSpec((None, 2, bf, MIX), lambda s, f: (s, 0, f, 0)),
        compiler_params=_cp(("arbitrary", "arbitrary"), 48),
    )(z, F, F, KF, KF)


def _hyinv_kernel(gc_ref, gs_ref, pr_ref, pi_ref, z_ref, gate_ref, skip_ref, o_ref):
    y = (jnp.dot(gc_ref[...], pr_ref[...], preferred_element_type=F32)
         + jnp.dot(gs_ref[...], pi_ref[...], preferred_element_type=F32))
    y = y + z_ref[...].astype(F32) * skip_ref[...]
    o_ref[...] = (gate_ref[...].astype(F32) * y).astype(o_ref.dtype)


def _hyinv(F, FT, P, z, zcol, u3, gcol, hy_bias, l, o, S, nseq, out_dtype):
    bn = min(512, S)
    nb = S // bn
    return pl.pallas_call(
        _hyinv_kernel, name="hy_inv_dft",
        out_shape=jax.ShapeDtypeStruct((nseq * S, MIX), out_dtype),
        grid=(nseq, nb),
        in_specs=[pl.BlockSpec((None, bn, S), lambda s, n: (0, n, 0)),
                  pl.BlockSpec((bn, S), lambda s, n: (n, 0)),
                  pl.BlockSpec((None, None, S, MIX), lambda s, n: (s, 0, 0, 0)),
                  pl.BlockSpec((None, None, S, MIX), lambda s, n: (s, 1, 0, 0)),
                  pl.BlockSpec((bn, MIX), lambda s, n: (s * nb + n, zcol)),
                  pl.BlockSpec((bn, MIX), lambda s, n: (s * nb + n, gcol)),
                  pl.BlockSpec((None, None, 1, MIX), lambda s, n: (l, o, 0, 0))],
        out_specs=pl.BlockSpec((bn, MIX), lambda s, n: (s * nb + n, 0)),
        compiler_params=_cp(("arbitrary", "arbitrary"), 48),
    )(F, FT, P, P, z, u3, hy_bias)


def _hyena(p, F, FT, KF, hy_short_w, hy_short_b3, hy_bias4, l, S, nseq, row_off):
    u3 = _hyshort(p, hy_short_w, hy_short_b3, l, S, nseq, row_off)
    P = _hyfwd(u3, 0, F, KF, 0, S, nseq)
    z1 = _hyinv(F, FT, P, u3, 0, u3, 1, hy_bias4, l, 0, S, nseq, BF16)
    P = _hyfwd(z1, 0, F, KF, 1, S, nseq)
    return _hyinv(F, FT, P, z1, 0, u3, 2, hy_bias4, l, 1, S, nseq, BF16)


def _rope_tables():
    d_axis = DH // 2
    inv = ROPE_BASE ** (-jnp.arange(0, d_axis, 2, dtype=F32) / d_axis)
    t = jnp.arange(SEQ)
    pos = jnp.stack([t // GRID_W, t % GRID_W], axis=-1).astype(F32)
    lane = np.arange(DH)
    ang = pos[:, lane // d_axis] * inv[lane % (d_axis // 2)][None, :]
    sign = np.where((lane % d_axis) < d_axis // 2, -1.0, 1.0).astype(np.float32)
    return jnp.cos(ang), jnp.sin(ang) * sign[None, :]


def _toeplitz_kernel(r_ref, e_ref, pick_ref, o_ref):
    hp = lax.Precision.HIGHEST
    toe = jnp.dot(r_ref[...], e_ref[...], precision=hp, preferred_element_type=F32)
    o_ref[...] = jnp.dot(pick_ref[...], toe, precision=hp, preferred_element_type=F32)


def _attn_bias(rpb_l):
    n_dr, n_dc = 2 * WIN_R - 1, 2 * WIN_C - 1
    d = lax.broadcasted_iota(jnp.int32, (128, GRID_W * GRID_W), 0)
    cw = lax.broadcasted_iota(jnp.int32, (128, GRID_W * GRID_W), 1)
    c, w = cw // GRID_W, cw % GRID_W
    c0 = jnp.clip(c - WIN_C // 2, 0, GRID_W - WIN_C)
    col_in = (w >= c0) & (w < c0 + WIN_C)
    sel = jnp.where(col_in, d == jnp.clip(w - c + (WIN_C - 1), 0, n_dc - 1), d == n_dc).astype(F32)
    tab = jnp.concatenate([rpb_l.astype(F32), jnp.full((HEADS, n_dr, 1), NEG, F32)], axis=2)
    tab = jnp.concatenate([tab, jnp.full((HEADS, 1, n_dc + 1), NEG, F32)], axis=1)
    tab = jnp.pad(tab, ((0, 0), (0, 128 - n_dr - 1), (0, 128 - n_dc - 1)))
    nblk = GRID_H // ATT_RQ
    pick = np.zeros((3 * ATT_RQ * ATT_KR, 128), np.float32)
    for vi, blk in enumerate((0, 1, nblk - 1)):
        start0 = int(np.clip(ATT_RQ * blk - WIN_R // 2, 0, GRID_H - ATT_KR))
        for j in range(ATT_RQ):
            r = ATT_RQ * blk + j
            rs = int(np.clip(r - WIN_R // 2, 0, GRID_H - WIN_R))
            for ir in range(ATT_KR):
                kr = start0 + ir
                dr = kr - r + (WIN_R - 1) if rs <= kr < rs + WIN_R else n_dr
                pick[(vi * ATT_RQ + j) * ATT_KR + ir, dr] = 1.0
    n_tile = pick.shape[0]
    tiles = pl.pallas_call(
        _toeplitz_kernel, name="na_bias_table",
        out_shape=jax.ShapeDtypeStruct((HEADS, n_tile, GRID_W * GRID_W), F32),
        grid=(HEADS,),
        in_specs=[pl.BlockSpec((None, 128, 128), lambda h: (h, 0, 0)),
                  pl.BlockSpec((128, GRID_W * GRID_W), lambda h: (0, 0)),
                  pl.BlockSpec((n_tile, 128), lambda h: (0, 0))],
        out_specs=pl.BlockSpec((None, n_tile, GRID_W * GRID_W), lambda h: (h, 0, 0)),
        compiler_params=_cp(("arbitrary",), 32),
    )(tab, sel, jnp.asarray(pick))
    tiles = tiles.reshape(HEADS, 3, ATT_RQ, ATT_KR, GRID_W, GRID_W)
    return tiles.transpose(0, 1, 2, 4, 3, 5).reshape(HEADS, 3, ATT_RQ * GRID_W, ATT_KR * GRID_W)


def _rope(x, cos, sin_signed):
    lane = lax.broadcasted_iota(jnp.int32, x.shape, 1)
    partner = jnp.where((lane & 32) == 0, pltpu.roll(x, DH - 32, axis=1), pltpu.roll(x, 32, axis=1))
    return x * cos + partner * sin_signed


def _attn_kernel(q_ref, k_ref, v_ref, kc_ref, vc_ref, cos_ref, sin_ref, bias_ref, o_ref,
                 qr_s, qp_s, kr_s, v_s):
    scale = DH ** -0.5
    q = q_ref[...].astype(F32) * scale
    cos, sin = cos_ref[...], sin_ref[...]
    qp_s[...] = q.astype(BF16)
    qr_s[...] = _rope(q, cos, sin).astype(BF16)
    kr_s[...] = _rope(k_ref[...].astype(F32), cos, sin).astype(BF16)
    v_s[...] = v_ref[...].astype(BF16)
    kc = kc_ref[...].astype(BF16)
    vc = vc_ref[...].astype(BF16)
    QB, KW = ATT_RQ * GRID_W, ATT_KR * GRID_W
    nblk = GRID_H // ATT_RQ
    tb = (((1,), (1,)), ((), ()))

    def body(i, carry):
        q0 = pl.multiple_of(i * QB, QB)
        start0 = jnp.clip(ATT_RQ * i - WIN_R // 2, 0, GRID_H - ATT_KR)
        k0 = pl.multiple_of(start0 * GRID_W, GRID_W)
        var = jnp.where(i == 0, 0, jnp.where(i == nblk - 1, 2, 1))
        s_loc = lax.dot_general(qr_s[pl.ds(q0, QB), :], kr_s[pl.ds(k0, KW), :], tb,
                                preferred_element_type=F32) + bias_ref[var]
        s_ctx = lax.dot_general(qp_s[pl.ds(q0, QB), :], kc, tb, preferred_element_type=F32)
        m = jnp.maximum(jnp.max(s_loc, axis=-1, keepdims=True), jnp.max(s_ctx, axis=-1, keepdims=True))
        p_loc = jnp.exp(s_loc - m)
        p_ctx = jnp.exp(s_ctx - m)
        den = jnp.sum(p_loc, axis=-1, keepdims=True) + jnp.sum(p_ctx, axis=-1, keepdims=True)
        o = (jnp.dot(p_loc.astype(BF16), v_s[pl.ds(k0, KW), :], preferred_element_type=F32)
             + jnp.dot(p_ctx.astype(BF16), vc, preferred_element_type=F32))
        o_ref[pl.ds(q0, QB), :] = (o / den).astype(o_ref.dtype)
        return carry

    lax.fori_loop(0, nblk, body, 0, unroll=2)


def _attn(p, pc, c_rowblk, c_kcol, c_vcol, cos, sin, bias):
    qc, kc_, vc_ = 3 * MIX // DH, 4 * MIX // DH, 5 * MIX // DH
    lat = lambda c: pl.BlockSpec((SEQ, DH), lambda h, b: (b, c + h))
    ctx = lambda c: pl.BlockSpec((CTX, DH), lambda h, b: (c_rowblk + b, c + h))
    tab = pl.BlockSpec((SEQ, DH), lambda h, b: (0, 0))
    QB, KW = ATT_RQ * GRID_W, ATT_KR * GRID_W
    return pl.pallas_call(
        _attn_kernel, name="na_attention",
        out_shape=jax.ShapeDtypeStruct((N_LAT, MIX), BF16),
        grid=(HEADS, BATCH),
        in_specs=[lat(qc), lat(kc_), lat(vc_), ctx(c_kcol), ctx(c_vcol), tab, tab,
                  pl.BlockSpec((None, 3, QB, KW), lambda h, b: (h, 0, 0, 0))],
        out_specs=pl.BlockSpec((SEQ, DH), lambda h, b: (b, h)),
        scratch_shapes=[pltpu.VMEM((SEQ, DH), BF16)] * 4,
        compiler_params=_cp(("arbitrary", "arbitrary"), 40),
    )(p, p, p, pc, pc, cos, sin, bias)


def _ctxattn_kernel(q_ref, k_ref, v_ref, o_ref):
    scale = DH ** -0.5
    q = (q_ref[...].astype(F32) * scale).astype(BF16)
    s = lax.dot_general(q, k_ref[...].astype(BF16), (((1,), (1,)), ((), ())), preferred_element_type=F32)
    m = jnp.max(s, axis=-1, keepdims=True)
    e = jnp.exp(s - m)
    den = jnp.sum(e, axis=-1, keepdims=True)
    o = jnp.dot(e.astype(BF16), v_ref[...].astype(BF16), preferred_element_type=F32)
    o_ref[...] = (o / den).astype(o_ref.dtype)


def _ctxattn(p):
    rb = N_LAT // CTX
    qc, kc_, vc_ = 3 * MIX // DH, 4 * MIX // DH, 5 * MIX // DH
    blk = lambda c: pl.BlockSpec((CTX, DH), lambda b, h: (rb + b, c + h))
    return pl.pallas_call(
        _ctxattn_kernel, name="ctx_attention",
        out_shape=jax.ShapeDtypeStruct((N_CTX, MIX), BF16),
        grid=(BATCH, HEADS),
        in_specs=[blk(qc), blk(kc_), blk(vc_)],
        out_specs=pl.BlockSpec((CTX, DH), lambda b, h: (b, h)),
        compiler_params=_cp(("arbitrary", "arbitrary"), 32),
    )(p, p, p)


def _pad2(a, r, c):
    return jnp.pad(a, ((0, r - a.shape[0]), (0, c - a.shape[1])))


def kernel(x, c, ctx, c_ctx, ada_w, ada_b, norm_g, ffn1_in, ffn1_out, ffn2_in, ffn2_out, w_in, pool_w, pool_scale, conv_dw_w, conv_dw_b, conv_ln_g, conv_ln_b, na_rpb, hy_short_w, hy_short_b, hy_w1, hy_b1, hy_freq1, hy_w2, hy_b2, hy_freq2, hy_w3, hy_decay, hy_bias, w_branch, w_out):
    X = (x.reshape(N_LAT, D), ctx.reshape(N_CTX, D))
    cc = jnp.concatenate([c, c_ctx[None], jnp.zeros((8 - BATCH - 1, D), F32)], axis=0)
    g3 = norm_g.reshape(DEPTH * 6, 1, D)
    ada_b3 = ada_b.reshape(DEPTH, 1, N_ADA * D)
    vec3 = lambda a: a.reshape(DEPTH, 1, a.shape[-1])
    pool_scale3, conv_dw_b3, conv_ln_g3, conv_ln_b3 = map(vec3, (pool_scale, conv_dw_b, conv_ln_g, conv_ln_b))
    hy_short_b3, hy_decay3 = vec3(hy_short_b), vec3(hy_decay)
    hy_bias4 = hy_bias.reshape(DEPTH, 2, 1, MIX)

    cos, sin = _rope_tables()
    bands = {SEQ: _pool_bands(SEQ), CTX: _pool_bands(CTX)}
    dft = {SEQ: _dft_mats(SEQ), CTX: _dft_mats(CTX)}
    feats = {SEQ: _hy_feats(SEQ), CTX: _hy_feats(CTX)}

    m3 = [_ada(cc, ada_w, ada_b3, l).reshape(8, 1, N_ADA * D) for l in range(DEPTH)]

    rows = N_ALL
    h = _normmod(X[0], X[1], g3, m3[0], 0, 0, 0)
    for l in range(DEPTH):
        last = l == DEPTH - 1
        m = m3[l]

        hid = _ffn_up(h, ffn1_in, l, rows)
        y = _matmul(hid, ffn1_out, l, rows, D, BF16, name="ffn_down", ni=8, bn=512)
        X, h = _resid(X, y, g3, m, l, 1, 2, 0.5, rows, nxt=(g3, m, l, 2, 3))

        if not last:
            p = _matmul(h, w_in, l, rows, PROJ_W, P_DT, name="proj_in", ni=8, bn=1024)
            pc, c_rowblk, c_kcol, c_vcol = p, N_LAT // CTX, 4 * MIX // DH, 5 * MIX // DH
        else:
            p = _matmul(h, w_in, l, N_LAT, PROJ_W, P_DT, name="proj_in", ni=8, bn=1024)
            pc = _matmul(h, w_in, l, N_CTX, 2 * MIX, P_DT, name="proj_ctx_kv", ni=2, bn=1024,
                         row_off=N_LAT, col_off=4 * MIX)
            c_rowblk, c_kcol, c_vcol = 0, 0, MIX // DH

        fpar = (_pad2(hy_w1[l], 128, 128), _pad2(hy_b1[l][None], 1, 128), _pad2(hy_freq1[l][None], 1, 128),
                _pad2(hy_w2[l], 128, 128), _pad2(hy_b2[l][None], 1, 128), _pad2(hy_freq2[l][None], 1, 128),
                _pad2(hy_w3[l], 128, 4 * MIX))

        def mixers(S, nseq, row_off):
            F, G = dft[S]
            KF = _kf(F, _hyfilt(feats[S], *fpar, hy_decay3, l, S), S)
            y0 = _pool(p, bands[S], pool_w, pool_scale3, l, S, nseq, row_off)
            y1 = _conv(p, conv_dw_w, conv_dw_b3, conv_ln_g3, conv_ln_b3, l, S, nseq, row_off)
            y3 = _hyena(p, F, G, KF, hy_short_w, hy_short_b3, hy_bias4, l, S, nseq, row_off)
            return y0, y1, y3

        y0, y1, y3 = mixers(SEQ, BATCH, 0)
        y2 = _attn(p, pc, c_rowblk, c_kcol, c_vcol, cos, sin, _attn_bias(na_rpb[l]))
        if not last:
            c0, c1, c3 = mixers(CTX, BATCH, N_LAT)
            ys_ctx = [c0, c1, _ctxattn(p), c3]
            mrows = rows
        else:
            ys_ctx = None
            mrows = N_LAT
        merged = _branch([y0, y1, y2, y3], ys_ctx, p, w_branch, l, mrows)
        y = _matmul(merged, w_out, l, mrows, D, BF16, name="proj_out", ni=8, bn=1024)
        X, h = _resid(X, y, g3, m, l, 3, 5, 1.0, mrows, nxt=(g3, m, l, 4, 6))
        rows = mrows

        hid = _ffn_up(h, ffn2_in, l, rows)
        y = _matmul(hid, ffn2_out, l, rows, D, BF16, name="ffn_down", ni=8, bn=512)
        if not last:
            X, h = _resid(X, y, g3, m, l, 5, 8, 0.5, rows, nxt=(g3, m3[l + 1], l + 1, 0, 0))
        else:
            X = _resid(X, y, g3, m, l, 5, 8, 0.5, rows)
    return X.reshape(BATCH, SEQ, D)
```

```python
import functools
import math

import numpy as np
import jax
import jax.numpy as jnp
from jax import lax
from jax.experimental import pallas as pl
from jax.experimental.pallas import tpu as pltpu

F32 = jnp.float32
BF16 = jnp.bfloat16

D = 4096
BATCH = 4
SEQ = 2048
DEPTH = 2
GRID_W = 64
GRID_H = SEQ // GRID_W
CTX = 256
D_FF = 6144
MIX = D // 4
N_BRANCH = 4
POOL_WINDOWS = (2, 4, 8, 16)
POOL_G = MIX // len(POOL_WINDOWS)
CONV_K = 31
DH = 128
HEADS = MIX // DH
WIN_R = 8
WIN_C = 16
ROPE_BASE = 10000.0
HY_EMB = 33
HY_HID = 64
N_ADA = 9
EPS = 1e-6
PROJ_W = 9 * MIX + N_BRANCH * D
G_OFF = 9 * MIX

N_LAT = BATCH * SEQ
N_CTX = BATCH * CTX
N_ALL = N_LAT + N_CTX

P_DT = BF16
NEG = -1e30

ATT_RQ = 4
ATT_KR = 12
MIB = 1024 * 1024
STREAM_VMEM_MIB = 60


def _cp(sem, vmem_mib):
    return pltpu.CompilerParams(dimension_semantics=sem, vmem_limit_bytes=vmem_mib * MIB)


def _modrow(i, bm):
    return jnp.minimum((i * bm) // SEQ, BATCH)


def _silu(v):
    return v * jax.nn.sigmoid(v)


def _ada_kernel(c_ref, w_ref, b_ref, o_ref):
    s = _silu(c_ref[...]).astype(BF16)
    o_ref[...] = jnp.dot(s, w_ref[...].astype(BF16), preferred_element_type=F32) + b_ref[...]


def _ada(cc, ada_w, ada_b3, l):
    bn = 512
    return pl.pallas_call(
        _ada_kernel, name="ada",
        out_shape=jax.ShapeDtypeStruct((8, N_ADA * D), F32),
        grid=(N_ADA * D // bn,),
        in_specs=[pl.BlockSpec((8, D), lambda j: (0, 0)),
                  pl.BlockSpec((None, D, bn), lambda j: (l, 0, j)),
                  pl.BlockSpec((None, 1, bn), lambda j: (l, 0, j))],
        out_specs=pl.BlockSpec((8, bn), lambda j: (0, j)),
        compiler_params=_cp(("arbitrary",), 40),
    )(cc, ada_w, ada_b3)


def _rms(v, g):
    return v * lax.rsqrt(jnp.mean(v * v, axis=-1, keepdims=True) + EPS) * g


def _stream_rows(lat_ref, ctx_ref):
    bm = lat_ref.shape[0]
    return jnp.where(pl.program_id(0) * bm < N_LAT, lat_ref[...], ctx_ref[...])


def _stream_specs(bm):
    nl = N_LAT // bm
    return [pl.BlockSpec((bm, D), lambda i: (jnp.minimum(i, nl - 1), 0)),
            pl.BlockSpec((bm, D), lambda i: (jnp.maximum(i - nl, 0), 0))]


def _normmod_kernel(x_ref, c_ref, g_ref, sh_ref, sc_ref, h_ref):
    y = _rms(_stream_rows(x_ref, c_ref), g_ref[...])
    h_ref[...] = (y * (1.0 + sc_ref[...]) + sh_ref[...]).astype(h_ref.dtype)


def _mod_spec(bm, k):
    return pl.BlockSpec((None, 1, D), lambda i: (_modrow(i, bm), 0, k))


def _g_spec(l, k):
    return pl.BlockSpec((None, 1, D), lambda i: (l * 6 + k, 0, 0))


def _normmod(x_lat, x_ctx, g3, m3, l, gk, mk):
    bm = 256
    return pl.pallas_call(
        _normmod_kernel, name="normmod",
        out_shape=jax.ShapeDtypeStruct((N_ALL, D), BF16),
        grid=(N_ALL // bm,),
        in_specs=_stream_specs(bm) + [_g_spec(l, gk), _mod_spec(bm, mk), _mod_spec(bm, mk + 1)],
        out_specs=pl.BlockSpec((bm, D), lambda i: (i, 0)),
        compiler_params=_cp(("arbitrary",), 32),
    )(x_lat, x_ctx, g3, m3, m3)


def _resid_kernel(*refs, coef, with_next, split_x):
    if split_x:
        x = _stream_rows(refs[0], refs[1])
        y_ref, gp_ref, gate_ref, *rest = refs[2:]
    else:
        x = refs[0][...]
        y_ref, gp_ref, gate_ref, *rest = refs[1:]
    xn = x + coef * gate_ref[...] * _rms(y_ref[...].astype(F32), gp_ref[...])
    if with_next:
        gn_ref, sh_ref, sc_ref, xo_ref, h_ref = rest
        xo_ref[...] = xn
        h_ref[...] = (_rms(xn, gn_ref[...]) * (1.0 + sc_ref[...]) + sh_ref[...]).astype(h_ref.dtype)
    else:
        (xo_ref,) = rest
        xo_ref[...] = xn


def _resid(x, y, g3, m3, l, gk_post, mk_gate, coef, rows, nxt=None):
    bm = 256
    split_x = isinstance(x, tuple)
    x_specs = _stream_specs(bm) if split_x else [pl.BlockSpec((bm, D), lambda i: (i, 0))]
    in_specs = x_specs + [pl.BlockSpec((bm, D), lambda i: (i, 0)), _g_spec(l, gk_post), _mod_spec(bm, mk_gate)]
    args = (list(x) if split_x else [x]) + [y, g3, m3]
    out_shape = [jax.ShapeDtypeStruct((rows, D), F32)]
    out_specs = [pl.BlockSpec((bm, D), lambda i: (i, 0))]
    if nxt is not None:
        g3n, m3n, ln, gkn, mkn = nxt
        in_specs += [_g_spec(ln, gkn), _mod_spec(bm, mkn), _mod_spec(bm, mkn + 1)]
        args += [g3n, m3n, m3n]
        out_shape.append(jax.ShapeDtypeStruct((rows, D), BF16))
        out_specs.append(pl.BlockSpec((bm, D), lambda i: (i, 0)))
    res = pl.pallas_call(
        functools.partial(_resid_kernel, coef=coef, with_next=nxt is not None, split_x=split_x), name="resid",
        out_shape=out_shape, grid=(rows // bm,), in_specs=in_specs, out_specs=out_specs,
        compiler_params=_cp(("arbitrary",), 48),
    )(*args)
    return res if nxt is not None else res[0]


def _stream_kernel(*refs, n_x, n_w, n_e, n_chunks, combine):
    xs = refs[:n_x]
    ws = refs[n_x:n_x + n_w]
    es = refs[n_x + n_w:n_x + n_w + n_e]
    o_ref = refs[n_x + n_w + n_e]
    slots = refs[n_x + n_w + n_e + 1:]
    slot_a, slot_b = slots[:n_w], slots[n_w:]
    j, i = pl.program_id(0), pl.program_id(1)

    def stage(dst):
        for w_ref, d in zip(ws, dst):
            ck = w_ref.shape[-2]
            r0 = pl.multiple_of(jnp.minimum(i, n_chunks - 1) * ck, ck)
            if len(w_ref.shape) == 3:
                d[:, pl.ds(r0, ck), :] = w_ref[...].astype(BF16)
            else:
                d[pl.ds(r0, ck), :] = w_ref[...].astype(BF16)

    def compute(src):
        o_ref[...] = combine(xs, src, es).astype(o_ref.dtype)

    @pl.when(j == 0)
    def _():
        stage(slot_a)
        o_ref[...] = jnp.zeros(o_ref.shape, o_ref.dtype)

    @pl.when(j % 2 == 1)
    def _():
        stage(slot_b)
        compute(slot_a)

    @pl.when((j > 0) & (j % 2 == 0))
    def _():
        stage(slot_a)
        compute(slot_b)


def _comb_plain(xs, ws, es):
    return jnp.dot(xs[0][...], ws[0][...], preferred_element_type=F32)


def _comb_swiglu(xs, ws, es):
    x = xs[0][...]
    a = jnp.dot(x, ws[0][...], preferred_element_type=F32)
    b = jnp.dot(x, ws[1][...], preferred_element_type=F32)
    return _silu(a) * b


def _comb_branch(xs, ws, es, n_lat_tiles=None):
    acc = None
    for br in range(N_BRANCH):
        y = xs[br][...]
        if n_lat_tiles is not None:
            y = jnp.where(pl.program_id(1) < n_lat_tiles, y, xs[N_BRANCH + br][...])
        t = jax.nn.sigmoid(es[br][...].astype(F32)) * jnp.dot(y, ws[0][br], preferred_element_type=F32)
        acc = t if acc is None else acc + t
    return acc


def _row_idx(j, i):
    return jnp.where(j == 0, 0, i)


def _stream_call(name, combine, xs, x_k, ws, w_specs, w_slots, es, e_specs, rows, n_out, out_dtype,
                 bm, bn, ro, vmem, n_chunks=None, x_specs=None):
    nj, ni = n_out // bn, rows // bm
    if x_specs is None:
        x_specs = [pl.BlockSpec((bm, x_k), lambda j, i: (_row_idx(j, i) + ro, 0))] * len(xs)
    return pl.pallas_call(
        functools.partial(_stream_kernel, n_x=len(xs), n_w=len(ws), n_e=len(es),
                          n_chunks=ni if n_chunks is None else n_chunks, combine=combine), name=name,
        out_shape=jax.ShapeDtypeStruct((rows, n_out), out_dtype),
        grid=(nj + 1, ni),
        in_specs=x_specs + w_specs + e_specs,
        out_specs=pl.BlockSpec((bm, bn), lambda j, i: (_row_idx(j, i), jnp.maximum(j - 1, 0))),
        scratch_shapes=w_slots + w_slots,
        compiler_params=_cp(("arbitrary", "arbitrary"), vmem),
    )(*xs, *ws, *es)


def _matmul(x, w, l, rows, n_out, out_dtype, *, name, ni, bn, row_off=0, col_off=0, vmem=STREAM_VMEM_MIB):
    K = x.shape[1]
    bm, ck, nj = rows // ni, K // ni, n_out // bn
    co = col_off // bn
    w_specs = [pl.BlockSpec((None, ck, bn), lambda j, i: (l, i, jnp.minimum(j, nj - 1) + co))]
    return _stream_call(name, _comb_plain, [x], K, [w], w_specs, [pltpu.VMEM((K, bn), BF16)], [], [],
                        rows, n_out, out_dtype, bm, bn, row_off // bm, vmem)


def _ffn_up(h, w_up, l, rows):
    ni, bn = 8, 512
    bm, ck, nj = rows // ni, D // ni, D_FF // bn
    w_specs = [pl.BlockSpec((None, ck, bn), lambda j, i: (l, i, jnp.minimum(j, nj - 1))),
               pl.BlockSpec((None, ck, bn), lambda j, i: (l, i, jnp.minimum(j, nj - 1) + nj))]
    return _stream_call("ffn_up", _comb_swiglu, [h], D, [w_up, w_up], w_specs, [pltpu.VMEM((D, bn), BF16)] * 2,
                        [], [], rows, D_FF, BF16, bm, bn, 0, STREAM_VMEM_MIB)


def _branch(ys, ys_ctx, p, w_branch, l, rows):
    n_chunks, bn, bm = 8, 512, N_LAT // 8
    ck, nj = MIX // n_chunks, D // bn
    w_specs = [pl.BlockSpec((None, N_BRANCH, ck, bn),
                            lambda j, i: (l, 0, jnp.minimum(i, n_chunks - 1), jnp.minimum(j, nj - 1)))]
    e_specs = [pl.BlockSpec((bm, bn), functools.partial(
        lambda j, i, br: (_row_idx(j, i), (G_OFF + br * D) // bn + jnp.maximum(j - 1, 0)), br=br))
        for br in range(N_BRANCH)]
    xs, x_specs, n_lat_tiles = list(ys), None, None
    if ys_ctx is not None:
        n_lat_tiles = N_LAT // bm
        xs = xs + list(ys_ctx)
        x_specs = ([pl.BlockSpec((bm, MIX), lambda j, i: (jnp.minimum(_row_idx(j, i), n_lat_tiles - 1), 0))] * N_BRANCH
                   + [pl.BlockSpec((bm, MIX), lambda j, i: (0, 0))] * N_BRANCH)
    return _stream_call("branch_merge", functools.partial(_comb_branch, n_lat_tiles=n_lat_tiles), xs, MIX,
                        [w_branch], w_specs, [pltpu.VMEM((N_BRANCH, MIX, bn), BF16)], [p] * N_BRANCH, e_specs,
                        rows, D, BF16, bm, bn, 0, STREAM_VMEM_MIB, n_chunks=n_chunks, x_specs=x_specs)


def _pool_bands(S):
    t = jnp.arange(S, dtype=jnp.int32)[:, None]
    m = jnp.arange(S, dtype=jnp.int32)[None, :]
    bands = []
    for w in POOL_WINDOWS:
        lo = jnp.clip(t - w // 2, 0, S)
        hi = jnp.clip(t + w // 2, 0, S)
        inside = (m >= lo) & (m < hi)
        band = jnp.where(inside, 1.0 / (hi - lo).astype(F32), 0.0) - (m == t).astype(F32)
        bands.append(band)
    return jnp.stack(bands).astype(BF16)


POOL_BLK = 256


def _pool_kernel(band_ref, p_ref, w_ref, sc_ref, o_ref):
    S = p_ref.shape[0]
    nb = S // POOL_BLK
    w = w_ref[...].astype(BF16)
    for r in range(nb):
        r0, r1 = r * POOL_BLK, (r + 1) * POOL_BLK
        k0, k1 = max(r - 1, 0) * POOL_BLK, min(r + 2, nb) * POOL_BLK
        y = jnp.dot(band_ref[r0:r1, k0:k1], p_ref[k0:k1, :].astype(BF16), preferred_element_type=F32)
        z = jnp.dot(y.astype(BF16), w, preferred_element_type=F32)
        o_ref[r0:r1, :] = (z * sc_ref[...]).astype(o_ref.dtype)


def _pool(p, bands, pool_w, pool_scale3, l, S, nseq, row_off):
    ro = row_off // S
    ng = len(POOL_WINDOWS)
    return pl.pallas_call(
        _pool_kernel, name="pool",
        out_shape=jax.ShapeDtypeStruct((nseq * S, MIX), BF16),
        grid=(ng, nseq),
        in_specs=[pl.BlockSpec((None, S, S), lambda g, s: (g, 0, 0)),
                  pl.BlockSpec((S, POOL_G), lambda g, s: (s + ro, g)),
                  pl.BlockSpec((None, None, POOL_G, POOL_G), lambda g, s: (l, g, 0, 0)),
                  pl.BlockSpec((None, 1, POOL_G), lambda g, s: (l, 0, g))],
        out_specs=pl.BlockSpec((S, POOL_G), lambda g, s: (s, g)),
        compiler_params=_cp(("arbitrary", "arbitrary"), 40),
    )(bands, p, pool_w, pool_scale3)


CONV_PAD = 16
CONV_CH = 32


def _conv_kernel(a_ref, g_ref, w_ref, b_ref, lg_ref, lb_ref, o_ref, u_ref, *, S):
    C = a_ref.shape[1]
    zeros = jnp.zeros((CONV_PAD, C), F32)
    u_ref[0:CONV_PAD, :] = zeros
    u_ref[S + CONV_PAD:S + 2 * CONV_PAD, :] = zeros
    SC = 128

    def stage(i, carry):
        r0 = pl.multiple_of(i * SC, SC)
        a = a_ref[pl.ds(r0, SC), :].astype(F32)
        g = g_ref[pl.ds(r0, SC), :].astype(F32)
        u_ref[pl.ds(r0 + CONV_PAD, SC), :] = a * jax.nn.sigmoid(g)
        return carry

    lax.fori_loop(0, S // SC, stage, 0)

    n = CONV_CH + 2 * CONV_PAD
    off = CONV_PAD - CONV_K // 2

    def body(i, carry):
        r0 = pl.multiple_of(i * CONV_CH, CONV_CH)
        win = u_ref[pl.ds(r0, n), :]
        acc = jnp.broadcast_to(b_ref[...], (CONV_CH, C))
        for sub in range(8):
            wsub = win if sub == 0 else pltpu.roll(win, n - sub, axis=0)
            for al in range(n // 8):
                k = 8 * al + sub - off
                if 0 <= k < CONV_K:
                    acc = acc + w_ref[k:k + 1, :] * wsub[8 * al:8 * al + CONV_CH]
        mu = jnp.mean(acc, axis=-1, keepdims=True)
        xc = acc - mu
        var = jnp.mean(xc * xc, axis=-1, keepdims=True)
        y = xc * lax.rsqrt(var + EPS) * lg_ref[...] + lb_ref[...]
        o_ref[pl.ds(r0, CONV_CH), :] = _silu(y).astype(o_ref.dtype)
        return carry

    lax.fori_loop(0, S // CONV_CH, body, 0)


def _conv(p, dw_w, dw_b3, ln_g3, ln_b3, l, S, nseq, row_off):
    ro = row_off // S
    vec = lambda: pl.BlockSpec((None, 1, MIX), lambda s: (l, 0, 0))
    return pl.pallas_call(
        functools.partial(_conv_kernel, S=S), name="conv_module",
        out_shape=jax.ShapeDtypeStruct((nseq * S, MIX), BF16),
        grid=(nseq,),
        in_specs=[pl.BlockSpec((S, MIX), lambda s: (s + ro, 1)),
                  pl.BlockSpec((S, MIX), lambda s: (s + ro, 2)),
                  pl.BlockSpec((None, CONV_K, MIX), lambda s: (l, 0, 0)),
                  vec(), vec(), vec()],
        out_specs=pl.BlockSpec((S, MIX), lambda s: (s, 0)),
        scratch_shapes=[pltpu.VMEM((S + 2 * CONV_PAD, MIX), F32)],
        compiler_params=_cp(("arbitrary",), 48),
    )(p, p, dw_w, dw_b3, ln_g3, ln_b3)


HY_PAD = 8
HY_CH = 64


def _hyshort_kernel(p_ref, w_ref, b_ref, o_ref, u_ref, *, S):
    C = p_ref.shape[1]
    zeros = jnp.zeros((HY_PAD, C), F32)
    u_ref[0:HY_PAD, :] = zeros
    u_ref[S + HY_PAD:S + 2 * HY_PAD, :] = zeros
    SC = 128

    def stage(i, carry):
        r0 = pl.multiple_of(i * SC, SC)
        u_ref[pl.ds(r0 + HY_PAD, SC), :] = p_ref[pl.ds(r0, SC), :].astype(F32)
        return carry

    lax.fori_loop(0, S // SC, stage, 0)
    n = HY_CH + 2 * HY_PAD

    def body(i, carry):
        r0 = pl.multiple_of(i * HY_CH, HY_CH)
        win = u_ref[pl.ds(r0, n), :]
        prev = pltpu.roll(win, n - (HY_PAD - 1), axis=0)[0:HY_CH]
        mid = win[HY_PAD:HY_PAD + HY_CH]
        nxt = pltpu.roll(win, n - 1, axis=0)[HY_PAD:HY_PAD + HY_CH]
        o_ref[pl.ds(r0, HY_CH), :] = (w_ref[0:1, :] * prev + w_ref[1:2, :] * mid + w_ref[2:3, :] * nxt
                                       + b_ref[...]).astype(o_ref.dtype)
        return carry

    lax.fori_loop(0, S // HY_CH, body, 0)


def _hyshort(p, sw, sb3, l, S, nseq, row_off):
    ro = row_off // S
    cb = (6 * MIX) // MIX
    return pl.pallas_call(
        functools.partial(_hyshort_kernel, S=S), name="hy_short",
        out_shape=jax.ShapeDtypeStruct((nseq * S, 3 * MIX), BF16),
        grid=(nseq, 3),
        in_specs=[pl.BlockSpec((S, MIX), lambda s, c: (s + ro, cb + c)),
                  pl.BlockSpec((None, 3, MIX), lambda s, c: (l, 0, c)),
                  pl.BlockSpec((None, 1, MIX), lambda s, c: (l, 0, c))],
        out_specs=pl.BlockSpec((S, MIX), lambda s, c: (s, c)),
        scratch_shapes=[pltpu.VMEM((S + 2 * HY_PAD, MIX), F32)],
        compiler_params=_cp(("arbitrary", "arbitrary"), 48),
    )(p, sw, sb3)


def _hy_feats(L):
    t = jnp.linspace(0.0, 1.0, L, dtype=F32)[:, None]
    omega = (2.0 * math.pi / L) * jnp.arange(L, dtype=F32)[:, None]
    bands = (HY_EMB - 1) // 2
    freqs = jnp.linspace(1e-4, bands - 1, bands, dtype=F32)[None, :]
    z = jnp.concatenate([t, jnp.cos(freqs * omega), -jnp.sin(freqs * omega)], axis=-1)
    return jnp.pad(z, ((0, 0), (0, 128 - HY_EMB)))


HYF_CH = 256


def _hyfilt_kernel(z_ref, w1_ref, b1_ref, f1_ref, w2_ref, b2_ref, f2_ref, w3_ref, dec_ref, o_ref, *, L):
    hp = lax.Precision.HIGHEST
    h = jnp.sin(f1_ref[...] * (jnp.dot(z_ref[...], w1_ref[...], precision=hp, preferred_element_type=F32)
                               + b1_ref[...]))
    h = jnp.sin(f2_ref[...] * (jnp.dot(h, w2_ref[...], precision=hp, preferred_element_type=F32) + b2_ref[...]))
    o = jnp.dot(h, w3_ref[...], precision=hp, preferred_element_type=F32)
    r0 = pl.program_id(0) * HYF_CH
    t = (lax.broadcasted_iota(jnp.int32, o.shape, 0) + r0).astype(F32) * (1.0 / (L - 1))
    o_ref[...] = o * jnp.exp(-t * jnp.abs(dec_ref[...]))


def _hyfilt(zf, w1p, b1p, f1p, w2p, b2p, f2p, w3p, dec3, l, L):
    nfc = 4 * MIX
    small = lambda shp: pl.BlockSpec(shp, lambda i: (0,) * len(shp))
    return pl.pallas_call(
        functools.partial(_hyfilt_kernel, L=L), name="hy_filter",
        out_shape=jax.ShapeDtypeStruct((L, nfc), F32),
        grid=(L // HYF_CH,),
        in_specs=[pl.BlockSpec((HYF_CH, 128), lambda i: (i, 0)), small((128, 128)), small((1, 128)), small((1, 128)),
                  small((128, 128)), small((1, 128)), small((1, 128)), small((128, nfc)),
                  pl.BlockSpec((None, 1, nfc), lambda i: (l, 0, 0))],
        out_specs=pl.BlockSpec((HYF_CH, nfc), lambda i: (i, 0)),
        compiler_params=_cp(("arbitrary",), 40),
    )(zf, w1p, b1p, f1p, w2p, b2p, f2p, w3p, dec3)


DFT_SPLIT = 64


def _dft_mats(L):
    N = 2 * L
    a = jnp.arange(L, dtype=jnp.int32)

    def trig(mult, n):
        idx = ((mult * jnp.arange(n, dtype=jnp.int32))[:, None] * a[None, :]) & (N - 1)
        ang = idx.astype(F32) * (2.0 * math.pi / N)
        return jnp.cos(ang), jnp.sin(ang)

    ch, sh = trig(DFT_SPLIT, L // DFT_SPLIT)
    cl, sl = trig(1, DFT_SPLIT)
    cs = (ch[:, None, :] * cl[None] - sh[:, None, :] * sl[None]).reshape(L, L)
    sn = (sh[:, None, :] * cl[None] + ch[:, None, :] * sl[None]).reshape(L, L)
    alt = (1 - 2 * (a & 1)).astype(F32)
    first = (a == 0)
    fs = jnp.where(first[:, None], alt[None, :], -sn)
    F = jnp.stack([cs, fs]).astype(BF16)
    return F, F[1].T


def _kf_kernel(fc_ref, fs_ref, hf_ref, hb_ref, o_ref, hs_s, hd_s, nyq_s):
    @pl.when(pl.program_id(2) == 0)
    def _():
        hf = hf_ref[...]
        rows = lax.broadcasted_iota(jnp.int32, hf.shape, 0)
        hb = jnp.where(rows == 0, 0.0, hb_ref[...])
        hs = hf + hb
        hs_s[...] = hs.astype(BF16)
        hd_s[...] = (hf - hb).astype(BF16)
        alt = (1 - 2 * (rows & 1)).astype(F32)
        nyq_s[...] = jnp.broadcast_to(jnp.sum(hs * alt, axis=0, keepdims=True), nyq_s.shape)

    kr = jnp.dot(fc_ref[...], hs_s[...], preferred_element_type=F32)
    ki = jnp.dot(fs_ref[...], hd_s[...], preferred_element_type=F32)
    bf = kr.shape[0]
    frow = lax.broadcasted_iota(jnp.int32, kr.shape, 0) + pl.program_id(2) * bf
    o_ref[0] = kr
    o_ref[1] = jnp.where(frow == 0, nyq_s[0:1, :], ki)


def _kf(F, hfilt, L):
    bf = min(512, L)
    cw = 512
    ncb = MIX // cw
    return pl.pallas_call(
        _kf_kernel, name="hy_filter_dft",
        out_shape=jax.ShapeDtypeStruct((2, 2, L, MIX), F32),
        grid=(2, ncb, L // bf),
        in_specs=[pl.BlockSpec((None, bf, L), lambda o, c, f: (0, f, 0)),
                  pl.BlockSpec((None, bf, L), lambda o, c, f: (1, f, 0)),
                  pl.BlockSpec((L, cw), lambda o, c, f: (0, o * ncb + c)),
                  pl.BlockSpec((L, cw), lambda o, c, f: (0, (2 + o) * ncb + c))],
        out_specs=pl.BlockSpec((None, 2, bf, cw), lambda o, c, f: (o, 0, f, c)),
        scratch_shapes=[pltpu.VMEM((L, cw), BF16)] * 2 + [pltpu.VMEM((8, cw), F32)],
        compiler_params=_cp(("arbitrary", "arbitrary", "arbitrary"), 48),
    )(F, F, hfilt, hfilt)


def _hyfwd_kernel(z_ref, fc_ref, fs_ref, kr_ref, ki_ref, o_ref):
    zb = z_ref[...]
    zr = jnp.dot(fc_ref[...], zb, preferred_element_type=F32)
    zi = jnp.dot(fs_ref[...], zb, preferred_element_type=F32)
    kr, ki = kr_ref[...], ki_ref[...]
    bf = zr.shape[0]
    frow = lax.broadcasted_iota(jnp.int32, zr.shape, 0) + pl.program_id(1) * bf
    dc = frow == 0
    n_circ = 2 * zb.shape[0]
    wf = jnp.where(dc, 1.0 / n_circ, 2.0 / n_circ)
    o_ref[0] = (jnp.where(dc, zr * kr, zr * kr - zi * ki) * wf).astype(o_ref.dtype)
    o_ref[1] = (jnp.where(dc, zi * ki, zr * ki + zi * kr) * wf).astype(o_ref.dtype)


def _hyfwd(z, zcol, F, KF, o, S, nseq):
    bf = min(512, S)
    return pl.pallas_call(
        _hyfwd_kernel, name="hy_fwd_dft",
        out_shape=jax.ShapeDtypeStruct((nseq, 2, S, MIX), BF16),
        grid=(nseq, S // bf),
        in_specs=[pl.BlockSpec((S, MIX), lambda s, f: (s, zcol)),
                  pl.BlockSpec((None, bf, S), lambda s, f: (0, f, 0)),
                  pl.BlockSpec((None, bf, S), lambda s, f: (1, f, 0)),
                  pl.BlockSpec((None, None, bf, MIX), lambda s, f: (o, 0, f, 0)),
                  pl.BlockSpec((None, None, bf, MIX), lambda s, f: (o, 1, f, 0))],
        out_specs=pl.BlockSpec((None, 2, bf, MIX), lambda s, f: (s, 0, f, 0)),
        compiler_params=_cp(("arbitrary", "arbitrary"), 48),
    )(z, F, F, KF, KF)


def _hyinv_kernel(gc_ref, gs_ref, pr_ref, pi_ref, z_ref, gate_ref, skip_ref, o_ref):
    y = (jnp.dot(gc_ref[...], pr_ref[...], preferred_element_type=F32)
         + jnp.dot(gs_ref[...], pi_ref[...], preferred_element_type=F32))
    y = y + z_ref[...].astype(F32) * skip_ref[...]
    o_ref[...] = (gate_ref[...].astype(F32) * y).astype(o_ref.dtype)


def _hyinv(F, FT, P, z, zcol, u3, gcol, hy_bias, l, o, S, nseq, out_dtype):
    bn = min(512, S)
    nb = S // bn
    return pl.pallas_call(
        _hyinv_kernel, name="hy_inv_dft",
        out_shape=jax.ShapeDtypeStruct((nseq * S, MIX), out_dtype),
        grid=(nseq, nb),
        in_specs=[pl.BlockSpec((None, bn, S), lambda s, n: (0, n, 0)),
                  pl.BlockSpec((bn, S), lambda s, n: (n, 0)),
                  pl.BlockSpec((None, None, S, MIX), lambda s, n: (s, 0, 0, 0)),
                  pl.BlockSpec((None, None, S, MIX), lambda s, n: (s, 1, 0, 0)),
                  pl.BlockSpec((bn, MIX), lambda s, n: (s * nb + n, zcol)),
                  pl.BlockSpec((bn, MIX), lambda s, n: (s * nb + n, gcol)),
                  pl.BlockSpec((None, None, 1, MIX), lambda s, n: (l, o, 0, 0))],
        out_specs=pl.BlockSpec((bn, MIX), lambda s, n: (s * nb + n, 0)),
        compiler_params=_cp(("arbitrary", "arbitrary"), 48),
    )(F, FT, P, P, z, u3, hy_bias)


def _hyena(p, F, FT, KF, hy_short_w, hy_short_b3, hy_bias4, l, S, nseq, row_off):
    u3 = _hyshort(p, hy_short_w, hy_short_b3, l, S, nseq, row_off)
    P = _hyfwd(u3, 0, F, KF, 0, S, nseq)
    z1 = _hyinv(F, FT, P, u3, 0, u3, 1, hy_bias4, l, 0, S, nseq, BF16)
    P = _hyfwd(z1, 0, F, KF, 1, S, nseq)
    return _hyinv(F, FT, P, z1, 0, u3, 2, hy_bias4, l, 1, S, nseq, BF16)


def _rope_tables():
    d_axis = DH // 2
    inv = ROPE_BASE ** (-jnp.arange(0, d_axis, 2, dtype=F32) / d_axis)
    t = jnp.arange(SEQ)
    pos = jnp.stack([t // GRID_W, t % GRID_W], axis=-1).astype(F32)
    lane = np.arange(DH)
    ang = pos[:, lane // d_axis] * inv[lane % (d_axis // 2)][None, :]
    sign = np.where((lane % d_axis) < d_axis // 2, -1.0, 1.0).astype(np.float32)
    return jnp.cos(ang), jnp.sin(ang) * sign[None, :]


def _toeplitz_kernel(r_ref, e_ref, o_ref):
    o_ref[...] = jnp.dot(r_ref[...], e_ref[...], precision=lax.Precision.HIGHEST, preferred_element_type=F32)


def _attn_bias(rpb_l):
    n_dr, n_dc = 2 * WIN_R - 1, 2 * WIN_C - 1
    d = lax.broadcasted_iota(jnp.int32, (128, GRID_W * GRID_W), 0)
    cw = lax.broadcasted_iota(jnp.int32, (128, GRID_W * GRID_W), 1)
    c, w = cw // GRID_W, cw % GRID_W
    c0 = jnp.clip(c - WIN_C // 2, 0, GRID_W - WIN_C)
    col_in = (w >= c0) & (w < c0 + WIN_C)
    sel = jnp.where(col_in, d == jnp.clip(w - c + (WIN_C - 1), 0, n_dc - 1), d == n_dc).astype(F32)
    tab = jnp.concatenate([rpb_l.reshape(HEADS * n_dr, n_dc).astype(F32),
                           jnp.full((HEADS * n_dr, 1), NEG, F32)], axis=1)
    tab = jnp.pad(tab, ((0, 128 - HEADS * n_dr), (0, 128 - n_dc - 1)))
    toe = pl.pallas_call(
        _toeplitz_kernel, name="na_bias_toeplitz",
        out_shape=jax.ShapeDtypeStruct((128, GRID_W * GRID_W), F32),
    )(tab, sel)
    toe = toe[:HEADS * n_dr].reshape(HEADS, n_dr, GRID_W, GRID_W)
    toe = toe.transpose(0, 2, 1, 3).reshape(HEADS, GRID_W, n_dr * GRID_W)
    nblk = GRID_H // ATT_RQ
    out = []
    for blk in (0, 1, nblk - 1):
        start0 = int(np.clip(ATT_RQ * blk - WIN_R // 2, 0, GRID_H - ATT_KR))
        for j in range(ATT_RQ):
            r = ATT_RQ * blk + j
            rs = int(np.clip(r - WIN_R // 2, 0, GRID_H - WIN_R))
            lead, dr0 = rs - start0, rs - r + (WIN_R - 1)
            tail = ATT_KR - WIN_R - lead
            parts = [toe[:, :, dr0 * GRID_W:(dr0 + WIN_R) * GRID_W]]
            if lead:
                parts.insert(0, jnp.full((HEADS, GRID_W, lead * GRID_W), NEG, F32))
            if tail:
                parts.append(jnp.full((HEADS, GRID_W, tail * GRID_W), NEG, F32))
            out.append(jnp.concatenate(parts, axis=-1))
    return jnp.concatenate(out, axis=1).reshape(HEADS, 3, ATT_RQ * GRID_W, ATT_KR * GRID_W)


def _rope(x, cos, sin_signed):
    lane = lax.broadcasted_iota(jnp.int32, x.shape, 1)
    partner = jnp.where((lane & 32) == 0, pltpu.roll(x, DH - 32, axis=1), pltpu.roll(x, 32, axis=1))
    return x * cos + partner * sin_signed


def _attn_kernel(q_ref, k_ref, v_ref, kc_ref, vc_ref, cos_ref, sin_ref, bias_ref, o_ref,
                 qr_s, qp_s, kr_s, v_s):
    scale = DH ** -0.5
    q = q_ref[...].astype(F32) * scale
    cos, sin = cos_ref[...], sin_ref[...]
    qp_s[...] = q.astype(BF16)
    qr_s[...] = _rope(q, cos, sin).astype(BF16)
    kr_s[...] = _rope(k_ref[...].astype(F32), cos, sin).astype(BF16)
    v_s[...] = v_ref[...].astype(BF16)
    kc = kc_ref[...].astype(BF16)
    vc = vc_ref[...].astype(BF16)
    QB, KW = ATT_RQ * GRID_W, ATT_KR * GRID_W
    nblk = GRID_H // ATT_RQ
    tb = (((1,), (1,)), ((), ()))

    def body(i, carry):
        q0 = pl.multiple_of(i * QB, QB)
        start0 = jnp.clip(ATT_RQ * i - WIN_R // 2, 0, GRID_H - ATT_KR)
        k0 = pl.multiple_of(start0 * GRID_W, GRID_W)
        var = jnp.where(i == 0, 0, jnp.where(i == nblk - 1, 2, 1))
        s_loc = lax.dot_general(qr_s[pl.ds(q0, QB), :], kr_s[pl.ds(k0, KW), :], tb,
                                preferred_element_type=F32) + bias_ref[var]
        s_ctx = lax.dot_general(qp_s[pl.ds(q0, QB), :], kc, tb, preferred_element_type=F32)
        m = jnp.maximum(jnp.max(s_loc, axis=-1, keepdims=True), jnp.max(s_ctx, axis=-1, keepdims=True))
        p_loc = jnp.exp(s_loc - m)
        p_ctx = jnp.exp(s_ctx - m)
        den = jnp.sum(p_loc, axis=-1, keepdims=True) + jnp.sum(p_ctx, axis=-1, keepdims=True)
        o = (jnp.dot(p_loc.astype(BF16), v_s[pl.ds(k0, KW), :], preferred_element_type=F32)
             + jnp.dot(p_ctx.astype(BF16), vc, preferred_element_type=F32))
        o_ref[pl.ds(q0, QB), :] = (o / den).astype(o_ref.dtype)
        return carry

    lax.fori_loop(0, nblk, body, 0, unroll=2)


def _attn(p, pc, c_rowblk, c_kcol, c_vcol, cos, sin, bias):
    qc, kc_, vc_ = 3 * MIX // DH, 4 * MIX // DH, 5 * MIX // DH
    lat = lambda c: pl.BlockSpec((SEQ, DH), lambda h, b: (b, c + h))
    ctx = lambda c: pl.BlockSpec((CTX, DH), lambda h, b: (c_rowblk + b, c + h))
    tab = pl.BlockSpec((SEQ, DH), lambda h, b: (0, 0))
    QB, KW = ATT_RQ * GRID_W, ATT_KR * GRID_W
    return pl.pallas_call(
        _attn_kernel, name="na_attention",
        out_shape=jax.ShapeDtypeStruct((N_LAT, MIX), BF16),
        grid=(HEADS, BATCH),
        in_specs=[lat(qc), lat(kc_), lat(vc_), ctx(c_kcol), ctx(c_vcol), tab, tab,
                  pl.BlockSpec((None, 3, QB, KW), lambda h, b: (h, 0, 0, 0))],
        out_specs=pl.BlockSpec((SEQ, DH), lambda h, b: (b, h)),
        scratch_shapes=[pltpu.VMEM((SEQ, DH), BF16)] * 4,
        compiler_params=_cp(("arbitrary", "arbitrary"), 40),
    )(p, p, p, pc, pc, cos, sin, bias)


def _ctxattn_kernel(q_ref, k_ref, v_ref, o_ref):
    scale = DH ** -0.5
    q = (q_ref[...].astype(F32) * scale).astype(BF16)
    s = lax.dot_general(q, k_ref[...].astype(BF16), (((1,), (1,)), ((), ())), preferred_element_type=F32)
    m = jnp.max(s, axis=-1, keepdims=True)
    e = jnp.exp(s - m)
    den = jnp.sum(e, axis=-1, keepdims=True)
    o = jnp.dot(e.astype(BF16), v_ref[...].astype(BF16), preferred_element_type=F32)
    o_ref[...] = (o / den).astype(o_ref.dtype)


def _ctxattn(p):
    rb = N_LAT // CTX
    qc, kc_, vc_ = 3 * MIX // DH, 4 * MIX // DH, 5 * MIX // DH
    blk = lambda c: pl.BlockSpec((CTX, DH), lambda b, h: (rb + b, c + h))
    return pl.pallas_call(
        _ctxattn_kernel, name="ctx_attention",
        out_shape=jax.ShapeDtypeStruct((N_CTX, MIX), BF16),
        grid=(BATCH, HEADS),
        in_specs=[blk(qc), blk(kc_), blk(vc_)],
        out_specs=pl.BlockSpec((CTX, DH), lambda b, h: (b, h)),
        compiler_params=_cp(("arbitrary", "arbitrary"), 32),
    )(p, p, p)


def _pad2(a, r, c):
    return jnp.pad(a, ((0, r - a.shape[0]), (0, c - a.shape[1])))


def kernel(x, c, ctx, c_ctx, ada_w, ada_b, norm_g, ffn1_in, ffn1_out, ffn2_in, ffn2_out, w_in, pool_w, pool_scale, conv_dw_w, conv_dw_b, conv_ln_g, conv_ln_b, na_rpb, hy_short_w, hy_short_b, hy_w1, hy_b1, hy_freq1, hy_w2, hy_b2, hy_freq2, hy_w3, hy_decay, hy_bias, w_branch, w_out):
    X = (x.reshape(N_LAT, D), ctx.reshape(N_CTX, D))
    cc = jnp.concatenate([c, c_ctx[None], jnp.zeros((8 - BATCH - 1, D), F32)], axis=0)
    g3 = norm_g.reshape(DEPTH * 6, 1, D)
    ada_b3 = ada_b.reshape(DEPTH, 1, N_ADA * D)
    vec3 = lambda a: a.reshape(DEPTH, 1, a.shape[-1])
    pool_scale3, conv_dw_b3, conv_ln_g3, conv_ln_b3 = map(vec3, (pool_scale, conv_dw_b, conv_ln_g, conv_ln_b))
    hy_short_b3, hy_decay3 = vec3(hy_short_b), vec3(hy_decay)
    hy_bias4 = hy_bias.reshape(DEPTH, 2, 1, MIX)

    cos, sin = _rope_tables()
    bands = {SEQ: _pool_bands(SEQ), CTX: _pool_bands(CTX)}
    dft = {SEQ: _dft_mats(SEQ), CTX: _dft_mats(CTX)}
    feats = {SEQ: _hy_feats(SEQ), CTX: _hy_feats(CTX)}

    m3 = [_ada(cc, ada_w, ada_b3, l).reshape(8, 1, N_ADA * D) for l in range(DEPTH)]

    rows = N_ALL
    h = _normmod(X[0], X[1], g3, m3[0], 0, 0, 0)
    for l in range(DEPTH):
        last = l == DEPTH - 1
        m = m3[l]

        hid = _ffn_up(h, ffn1_in, l, rows)
        y = _matmul(hid, ffn1_out, l, rows, D, BF16, name="ffn_down", ni=8, bn=512)
        X, h = _resid(X, y, g3, m, l, 1, 2, 0.5, rows, nxt=(g3, m, l, 2, 3))

        if not last:
            p = _matmul(h, w_in, l, rows, PROJ_W, P_DT, name="proj_in", ni=8, bn=1024)
            pc, c_rowblk, c_kcol, c_vcol = p, N_LAT // CTX, 4 * MIX // DH, 5 * MIX // DH
        else:
            p = _matmul(h, w_in, l, N_LAT, PROJ_W, P_DT, name="proj_in", ni=8, bn=1024)
            pc = _matmul(h, w_in, l, N_CTX, 2 * MIX, P_DT, name="proj_ctx_kv", ni=2, bn=1024,
                         row_off=N_LAT, col_off=4 * MIX)
            c_rowblk, c_kcol, c_vcol = 0, 0, MIX // DH

        fpar = (_pad2(hy_w1[l], 128, 128), _pad2(hy_b1[l][None], 1, 128), _pad2(hy_freq1[l][None], 1, 128),
                _pad2(hy_w2[l], 128, 128), _pad2(hy_b2[l][None], 1, 128), _pad2(hy_freq2[l][None], 1, 128),
                _pad2(hy_w3[l], 128, 4 * MIX))

        def mixers(S, nseq, row_off):
            F, G = dft[S]
            KF = _kf(F, _hyfilt(feats[S], *fpar, hy_decay3, l, S), S)
            y0 = _pool(p, bands[S], pool_w, pool_scale3, l, S, nseq, row_off)
            y1 = _conv(p, conv_dw_w, conv_dw_b3, conv_ln_g3, conv_ln_b3, l, S, nseq, row_off)
            y3 = _hyena(p, F, G, KF, hy_short_w, hy_short_b3, hy_bias4, l, S, nseq, row_off)
            return y0, y1, y3

        y0, y1, y3 = mixers(SEQ, BATCH, 0)
        y2 = _attn(p, pc, c_rowblk, c_kcol, c_vcol, cos, sin, _attn_bias(na_rpb[l]))
        if not last:
            c0, c1, c3 = mixers(CTX, BATCH, N_LAT)
            ys_ctx = [c0, c1, _ctxattn(p), c3]
            mrows = rows
        else:
            ys_ctx = None
            mrows = N_LAT
        merged = _branch([y0, y1, y2, y3], ys_ctx, p, w_branch, l, mrows)
        y = _matmul(merged, w_out, l, mrows, D, BF16, name="proj_out", ni=8, bn=1024)
        X, h = _resid(X, y, g3, m, l, 3, 5, 1.0, mrows, nxt=(g3, m, l, 4, 6))
        rows = mrows

        hid = _ffn_up(h, ffn2_in, l, rows)
        y = _matmul(hid, ffn2_out, l, rows, D, BF16, name="ffn_down", ni=8, bn=512)
        if not last:
            X, h = _resid(X, y, g3, m, l, 5, 8, 0.5, rows, nxt=(g3, m3[l + 1], l + 1, 0, 0))
        else:
            X = _resid(X, y, g3, m, l, 5, 8, 0.5, rows)
    return X.reshape(BATCH, SEQ, D)
```

```python
import functools
import math

import numpy as np
import jax
import jax.numpy as jnp
from jax import lax
from jax.experimental import pallas as pl
from jax.experimental.pallas import tpu as pltpu

F32 = jnp.float32
BF16 = jnp.bfloat16

D = 4096
BATCH = 4
SEQ = 2048
DEPTH = 2
GRID_W = 64
GRID_H = SEQ // GRID_W
CTX = 256
D_FF = 6144
MIX = D // 4
N_BRANCH = 4
POOL_WINDOWS = (2, 4, 8, 16)
POOL_G = MIX // len(POOL_WINDOWS)
CONV_K = 31
DH = 128
HEADS = MIX // DH
WIN_R = 8
WIN_C = 16
ROPE_BASE = 10000.0
HY_EMB = 33
HY_HID = 64
N_ADA = 9
EPS = 1e-6
PROJ_W = 9 * MIX + N_BRANCH * D
G_OFF = 9 * MIX

N_LAT = BATCH * SEQ
N_CTX = BATCH * CTX
N_ALL = N_LAT + N_CTX

P_DT = BF16
NEG = -1e30

ATT_RQ = 4
ATT_KR = 12
MIB = 1024 * 1024
STREAM_VMEM_MIB = 60


def _cp(sem, vmem_mib):
    return pltpu.CompilerParams(dimension_semantics=sem, vmem_limit_bytes=vmem_mib * MIB)


def _modrow(i, bm):
    return jnp.minimum((i * bm) // SEQ, BATCH)


def _silu(v):
    return v * jax.nn.sigmoid(v)


def _ada_kernel(c_ref, w_ref, b_ref, o_ref):
    s = _silu(c_ref[...]).astype(BF16)
    o_ref[...] = jnp.dot(s, w_ref[...].astype(BF16), preferred_element_type=F32) + b_ref[...]


def _ada(cc, ada_w, ada_b3, l):
    bn = 512
    return pl.pallas_call(
        _ada_kernel, name="ada",
        out_shape=jax.ShapeDtypeStruct((8, N_ADA * D), F32),
        grid=(N_ADA * D // bn,),
        in_specs=[pl.BlockSpec((8, D), lambda j: (0, 0)),
                  pl.BlockSpec((None, D, bn), lambda j: (l, 0, j)),
                  pl.BlockSpec((None, 1, bn), lambda j: (l, 0, j))],
        out_specs=pl.BlockSpec((8, bn), lambda j: (0, j)),
        compiler_params=_cp(("arbitrary",), 40),
    )(cc, ada_w, ada_b3)


def _rms(v, g):
    return v * lax.rsqrt(jnp.mean(v * v, axis=-1, keepdims=True) + EPS) * g


def _stream_rows(lat_ref, ctx_ref):
    bm = lat_ref.shape[0]
    return jnp.where(pl.program_id(0) * bm < N_LAT, lat_ref[...], ctx_ref[...])


def _stream_specs(bm):
    nl = N_LAT // bm
    return [pl.BlockSpec((bm, D), lambda i: (jnp.minimum(i, nl - 1), 0)),
            pl.BlockSpec((bm, D), lambda i: (jnp.maximum(i - nl, 0), 0))]


def _normmod_kernel(x_ref, c_ref, g_ref, sh_ref, sc_ref, h_ref):
    y = _rms(_stream_rows(x_ref, c_ref), g_ref[...])
    h_ref[...] = (y * (1.0 + sc_ref[...]) + sh_ref[...]).astype(h_ref.dtype)


def _mod_spec(bm, k):
    return pl.BlockSpec((None, 1, D), lambda i: (_modrow(i, bm), 0, k))


def _g_spec(l, k):
    return pl.BlockSpec((None, 1, D), lambda i: (l * 6 + k, 0, 0))


def _normmod(x_lat, x_ctx, g3, m3, l, gk, mk):
    bm = 256
    return pl.pallas_call(
        _normmod_kernel, name="normmod",
        out_shape=jax.ShapeDtypeStruct((N_ALL, D), BF16),
        grid=(N_ALL // bm,),
        in_specs=_stream_specs(bm) + [_g_spec(l, gk), _mod_spec(bm, mk), _mod_spec(bm, mk + 1)],
        out_specs=pl.BlockSpec((bm, D), lambda i: (i, 0)),
        compiler_params=_cp(("arbitrary",), 32),
    )(x_lat, x_ctx, g3, m3, m3)


def _resid_kernel(*refs, coef, with_next, split_x):
    if split_x:
        x = _stream_rows(refs[0], refs[1])
        y_ref, gp_ref, gate_ref, *rest = refs[2:]
    else:
        x = refs[0][...]
        y_ref, gp_ref, gate_ref, *rest = refs[1:]
    xn = x + coef * gate_ref[...] * _rms(y_ref[...].astype(F32), gp_ref[...])
    if with_next:
        gn_ref, sh_ref, sc_ref, xo_ref, h_ref = rest
        xo_ref[...] = xn
        h_ref[...] = (_rms(xn, gn_ref[...]) * (1.0 + sc_ref[...]) + sh_ref[...]).astype(h_ref.dtype)
    else:
        (xo_ref,) = rest
        xo_ref[...] = xn


def _resid(x, y, g3, m3, l, gk_post, mk_gate, coef, rows, nxt=None):
    bm = 256
    split_x = isinstance(x, tuple)
    x_specs = _stream_specs(bm) if split_x else [pl.BlockSpec((bm, D), lambda i: (i, 0))]
    in_specs = x_specs + [pl.BlockSpec((bm, D), lambda i: (i, 0)), _g_spec(l, gk_post), _mod_spec(bm, mk_gate)]
    args = (list(x) if split_x else [x]) + [y, g3, m3]
    out_shape = [jax.ShapeDtypeStruct((rows, D), F32)]
    out_specs = [pl.BlockSpec((bm, D), lambda i: (i, 0))]
    if nxt is not None:
        g3n, m3n, ln, gkn, mkn = nxt
        in_specs += [_g_spec(ln, gkn), _mod_spec(bm, mkn), _mod_spec(bm, mkn + 1)]
        args += [g3n, m3n, m3n]
        out_shape.append(jax.ShapeDtypeStruct((rows, D), BF16))
        out_specs.append(pl.BlockSpec((bm, D), lambda i: (i, 0)))
    res = pl.pallas_call(
        functools.partial(_resid_kernel, coef=coef, with_next=nxt is not None, split_x=split_x), name="resid",
        out_shape=out_shape, grid=(rows // bm,), in_specs=in_specs, out_specs=out_specs,
        compiler_params=_cp(("arbitrary",), 48),
    )(*args)
    return res if nxt is not None else res[0]


def _stream_kernel(*refs, n_x, n_w, n_e, n_chunks, combine):
    xs = refs[:n_x]
    ws = refs[n_x:n_x + n_w]
    es = refs[n_x + n_w:n_x + n_w + n_e]
    o_ref = refs[n_x + n_w + n_e]
    slots = refs[n_x + n_w + n_e + 1:]
    slot_a, slot_b = slots[:n_w], slots[n_w:]
    j, i = pl.program_id(0), pl.program_id(1)

    def stage(dst):
        for w_ref, d in zip(ws, dst):
            ck = w_ref.shape[-2]
            r0 = pl.multiple_of(jnp.minimum(i, n_chunks - 1) * ck, ck)
            if len(w_ref.shape) == 3:
                d[:, pl.ds(r0, ck), :] = w_ref[...].astype(BF16)
            else:
                d[pl.ds(r0, ck), :] = w_ref[...].astype(BF16)

    def compute(src):
        o_ref[...] = combine(xs, src, es).astype(o_ref.dtype)

    @pl.when(j == 0)
    def _():
        stage(slot_a)
        o_ref[...] = jnp.zeros(o_ref.shape, o_ref.dtype)

    @pl.when(j % 2 == 1)
    def _():
        stage(slot_b)
        compute(slot_a)

    @pl.when((j > 0) & (j % 2 == 0))
    def _():
        stage(slot_a)
        compute(slot_b)


def _comb_plain(xs, ws, es):
    return jnp.dot(xs[0][...], ws[0][...], preferred_element_type=F32)


def _comb_swiglu(xs, ws, es):
    x = xs[0][...]
    a = jnp.dot(x, ws[0][...], preferred_element_type=F32)
    b = jnp.dot(x, ws[1][...], preferred_element_type=F32)
    return _silu(a) * b


def _comb_branch(xs, ws, es, n_lat_tiles=None):
    acc = None
    for br in range(N_BRANCH):
        y = xs[br][...]
        if n_lat_tiles is not None:
            y = jnp.where(pl.program_id(1) < n_lat_tiles, y, xs[N_BRANCH + br][...])
        t = jax.nn.sigmoid(es[br][...].astype(F32)) * jnp.dot(y, ws[0][br], preferred_element_type=F32)
        acc = t if acc is None else acc + t
    return acc


def _row_idx(j, i):
    return jnp.where(j == 0, 0, i)


def _stream_call(name, combine, xs, x_k, ws, w_specs, w_slots, es, e_specs, rows, n_out, out_dtype,
                 bm, bn, ro, vmem, n_chunks=None, x_specs=None):
    nj, ni = n_out // bn, rows // bm
    if x_specs is None:
        x_specs = [pl.BlockSpec((bm, x_k), lambda j, i: (_row_idx(j, i) + ro, 0))] * len(xs)
    return pl.pallas_call(
        functools.partial(_stream_kernel, n_x=len(xs), n_w=len(ws), n_e=len(es),
                          n_chunks=ni if n_chunks is None else n_chunks, combine=combine), name=name,
        out_shape=jax.ShapeDtypeStruct((rows, n_out), out_dtype),
        grid=(nj + 1, ni),
        in_specs=x_specs + w_specs + e_specs,
        out_specs=pl.BlockSpec((bm, bn), lambda j, i: (_row_idx(j, i), jnp.maximum(j - 1, 0))),
        scratch_shapes=w_slots + w_slots,
        compiler_params=_cp(("arbitrary", "arbitrary"), vmem),
    )(*xs, *ws, *es)


def _matmul(x, w, l, rows, n_out, out_dtype, *, name, ni, bn, row_off=0, col_off=0, vmem=STREAM_VMEM_MIB):
    K = x.shape[1]
    bm, ck, nj = rows // ni, K // ni, n_out // bn
    co = col_off // bn
    w_specs = [pl.BlockSpec((None, ck, bn), lambda j, i: (l, i, jnp.minimum(j, nj - 1) + co))]
    return _stream_call(name, _comb_plain, [x], K, [w], w_specs, [pltpu.VMEM((K, bn), BF16)], [], [],
                        rows, n_out, out_dtype, bm, bn, row_off // bm, vmem)


def _ffn_up(h, w_up, l, rows):
    ni, bn = 8, 512
    bm, ck, nj = rows // ni, D // ni, D_FF // bn
    w_specs = [pl.BlockSpec((None, ck, bn), lambda j, i: (l, i, jnp.minimum(j, nj - 1))),
               pl.BlockSpec((None, ck, bn), lambda j, i: (l, i, jnp.minimum(j, nj - 1) + nj))]
    return _stream_call("ffn_up", _comb_swiglu, [h], D, [w_up, w_up], w_specs, [pltpu.VMEM((D, bn), BF16)] * 2,
                        [], [], rows, D_FF, BF16, bm, bn, 0, STREAM_VMEM_MIB)


def _branch(ys, ys_ctx, p, w_branch, l, rows):
    n_chunks, bn, bm = 8, 512, N_LAT // 8
    ck, nj = MIX // n_chunks, D // bn
    w_specs = [pl.BlockSpec((None, N_BRANCH, ck, bn),
                            lambda j, i: (l, 0, jnp.minimum(i, n_chunks - 1), jnp.minimum(j, nj - 1)))]
    e_specs = [pl.BlockSpec((bm, bn), functools.partial(
        lambda j, i, br: (_row_idx(j, i), (G_OFF + br * D) // bn + jnp.maximum(j - 1, 0)), br=br))
        for br in range(N_BRANCH)]
    xs, x_specs, n_lat_tiles = list(ys), None, None
    if ys_ctx is not None:
        n_lat_tiles = N_LAT // bm
        xs = xs + list(ys_ctx)
        x_specs = ([pl.BlockSpec((bm, MIX), lambda j, i: (jnp.minimum(_row_idx(j, i), n_lat_tiles - 1), 0))] * N_BRANCH
                   + [pl.BlockSpec((bm, MIX), lambda j, i: (0, 0))] * N_BRANCH)
    return _stream_call("branch_merge", functools.partial(_comb_branch, n_lat_tiles=n_lat_tiles), xs, MIX,
                        [w_branch], w_specs, [pltpu.VMEM((N_BRANCH, MIX, bn), BF16)], [p] * N_BRANCH, e_specs,
                        rows, D, BF16, bm, bn, 0, STREAM_VMEM_MIB, n_chunks=n_chunks, x_specs=x_specs)


def _pool_bands(S):
    t = jnp.arange(S, dtype=jnp.int32)[:, None]
    m = jnp.arange(S, dtype=jnp.int32)[None, :]
    bands = []
    for w in POOL_WINDOWS:
        lo = jnp.clip(t - w // 2, 0, S)
        hi = jnp.clip(t + w // 2, 0, S)
        inside = (m >= lo) & (m < hi)
        band = jnp.where(inside, 1.0 / (hi - lo).astype(F32), 0.0) - (m == t).astype(F32)
        bands.append(band)
    return jnp.stack(bands).astype(BF16)


POOL_BLK = 256


def _pool_kernel(band_ref, p_ref, w_ref, sc_ref, o_ref):
    S = p_ref.shape[0]
    nb = S // POOL_BLK
    w = w_ref[...].astype(BF16)
    for r in range(nb):
        r0, r1 = r * POOL_BLK, (r + 1) * POOL_BLK
        k0, k1 = max(r - 1, 0) * POOL_BLK, min(r + 2, nb) * POOL_BLK
        y = jnp.dot(band_ref[r0:r1, k0:k1], p_ref[k0:k1, :].astype(BF16), preferred_element_type=F32)
        z = jnp.dot(y.astype(BF16), w, preferred_element_type=F32)
        o_ref[r0:r1, :] = (z * sc_ref[...]).astype(o_ref.dtype)


def _pool(p, bands, pool_w, pool_scale3, l, S, nseq, row_off):
    ro = row_off // S
    ng = len(POOL_WINDOWS)
    return pl.pallas_call(
        _pool_kernel, name="pool",
        out_shape=jax.ShapeDtypeStruct((nseq * S, MIX), BF16),
        grid=(ng, nseq),
        in_specs=[pl.BlockSpec((None, S, S), lambda g, s: (g, 0, 0)),
                  pl.BlockSpec((S, POOL_G), lambda g, s: (s + ro, g)),
                  pl.BlockSpec((None, None, POOL_G, POOL_G), lambda g, s: (l, g, 0, 0)),
                  pl.BlockSpec((None, 1, POOL_G), lambda g, s: (l, 0, g))],
        out_specs=pl.BlockSpec((S, POOL_G), lambda g, s: (s, g)),
        compiler_params=_cp(("arbitrary", "arbitrary"), 40),
    )(bands, p, pool_w, pool_scale3)


CONV_PAD = 16
CONV_CH = 32


def _conv_kernel(a_ref, g_ref, w_ref, b_ref, lg_ref, lb_ref, o_ref, u_ref, *, S):
    C = a_ref.shape[1]
    zeros = jnp.zeros((CONV_PAD, C), F32)
    u_ref[0:CONV_PAD, :] = zeros
    u_ref[S + CONV_PAD:S + 2 * CONV_PAD, :] = zeros
    SC = 128

    def stage(i, carry):
        r0 = pl.multiple_of(i * SC, SC)
        a = a_ref[pl.ds(r0, SC), :].astype(F32)
        g = g_ref[pl.ds(r0, SC), :].astype(F32)
        u_ref[pl.ds(r0 + CONV_PAD, SC), :] = a * jax.nn.sigmoid(g)
        return carry

    lax.fori_loop(0, S // SC, stage, 0)

    n = CONV_CH + 2 * CONV_PAD
    off = CONV_PAD - CONV_K // 2

    def body(i, carry):
        r0 = pl.multiple_of(i * CONV_CH, CONV_CH)
        win = u_ref[pl.ds(r0, n), :]
        acc = jnp.broadcast_to(b_ref[...], (CONV_CH, C))
        for sub in range(8):
            wsub = win if sub == 0 else pltpu.roll(win, n - sub, axis=0)
            for al in range(n // 8):
                k = 8 * al + sub - off
                if 0 <= k < CONV_K:
                    acc = acc + w_ref[k:k + 1, :] * wsub[8 * al:8 * al + CONV_CH]
        mu = jnp.mean(acc, axis=-1, keepdims=True)
        xc = acc - mu
        var = jnp.mean(xc * xc, axis=-1, keepdims=True)
        y = xc * lax.rsqrt(var + EPS) * lg_ref[...] + lb_ref[...]
        o_ref[pl.ds(r0, CONV_CH), :] = _silu(y).astype(o_ref.dtype)
        return carry

    lax.fori_loop(0, S // CONV_CH, body, 0)


def _conv(p, dw_w, dw_b3, ln_g3, ln_b3, l, S, nseq, row_off):
    ro = row_off // S
    vec = lambda: pl.BlockSpec((None, 1, MIX), lambda s: (l, 0, 0))
    return pl.pallas_call(
        functools.partial(_conv_kernel, S=S), name="conv_module",
        out_shape=jax.ShapeDtypeStruct((nseq * S, MIX), BF16),
        grid=(nseq,),
        in_specs=[pl.BlockSpec((S, MIX), lambda s: (s + ro, 1)),
                  pl.BlockSpec((S, MIX), lambda s: (s + ro, 2)),
                  pl.BlockSpec((None, CONV_K, MIX), lambda s: (l, 0, 0)),
                  vec(), vec(), vec()],
        out_specs=pl.BlockSpec((S, MIX), lambda s: (s, 0)),
        scratch_shapes=[pltpu.VMEM((S + 2 * CONV_PAD, MIX), F32)],
        compiler_params=_cp(("arbitrary",), 48),
    )(p, p, dw_w, dw_b3, ln_g3, ln_b3)


HY_PAD = 8
HY_CH = 64


def _hyshort_kernel(p_ref, w_ref, b_ref, o_ref, u_ref, *, S):
    C = p_ref.shape[1]
    zeros = jnp.zeros((HY_PAD, C), F32)
    u_ref[0:HY_PAD, :] = zeros
    u_ref[S + HY_PAD:S + 2 * HY_PAD, :] = zeros
    SC = 128

    def stage(i, carry):
        r0 = pl.multiple_of(i * SC, SC)
        u_ref[pl.ds(r0 + HY_PAD, SC), :] = p_ref[pl.ds(r0, SC), :].astype(F32)
        return carry

    lax.fori_loop(0, S // SC, stage, 0)
    n = HY_CH + 2 * HY_PAD

    def body(i, carry):
        r0 = pl.multiple_of(i * HY_CH, HY_CH)
        win = u_ref[pl.ds(r0, n), :]
        prev = pltpu.roll(win, n - (HY_PAD - 1), axis=0)[0:HY_CH]
        mid = win[HY_PAD:HY_PAD + HY_CH]
        nxt = pltpu.roll(win, n - 1, axis=0)[HY_PAD:HY_PAD + HY_CH]
        o_ref[pl.ds(r0, HY_CH), :] = (w_ref[0:1, :] * prev + w_ref[1:2, :] * mid + w_ref[2:3, :] * nxt
                                       + b_ref[...]).astype(o_ref.dtype)
        return carry

    lax.fori_loop(0, S // HY_CH, body, 0)


def _hyshort(p, sw, sb3, l, S, nseq, row_off):
    ro = row_off // S
    cb = (6 * MIX) // MIX
    return pl.pallas_call(
        functools.partial(_hyshort_kernel, S=S), name="hy_short",
        out_shape=jax.ShapeDtypeStruct((nseq * S, 3 * MIX), BF16),
        grid=(nseq, 3),
        in_specs=[pl.BlockSpec((S, MIX), lambda s, c: (s + ro, cb + c)),
                  pl.BlockSpec((None, 3, MIX), lambda s, c: (l, 0, c)),
                  pl.BlockSpec((None, 1, MIX), lambda s, c: (l, 0, c))],
        out_specs=pl.BlockSpec((S, MIX), lambda s, c: (s, c)),
        scratch_shapes=[pltpu.VMEM((S + 2 * HY_PAD, MIX), F32)],
        compiler_params=_cp(("arbitrary", "arbitrary"), 48),
    )(p, sw, sb3)


def _hy_feats(L):
    t = jnp.linspace(0.0, 1.0, L, dtype=F32)[:, None]
    omega = (2.0 * math.pi / L) * jnp.arange(L, dtype=F32)[:, None]
    bands = (HY_EMB - 1) // 2
    freqs = jnp.linspace(1e-4, bands - 1, bands, dtype=F32)[None, :]
    z = jnp.concatenate([t, jnp.cos(freqs * omega), -jnp.sin(freqs * omega)], axis=-1)
    return jnp.pad(z, ((0, 0), (0, 128 - HY_EMB)))


HYF_CH = 256


def _hyfilt_kernel(z_ref, w1_ref, b1_ref, f1_ref, w2_ref, b2_ref, f2_ref, w3_ref, dec_ref, o_ref, *, L):
    hp = lax.Precision.HIGHEST
    h = jnp.sin(f1_ref[...] * (jnp.dot(z_ref[...], w1_ref[...], precision=hp, preferred_element_type=F32)
                               + b1_ref[...]))
    h = jnp.sin(f2_ref[...] * (jnp.dot(h, w2_ref[...], precision=hp, preferred_element_type=F32) + b2_ref[...]))
    o = jnp.dot(h, w3_ref[...], precision=hp, preferred_element_type=F32)
    r0 = pl.program_id(0) * HYF_CH
    t = (lax.broadcasted_iota(jnp.int32, o.shape, 0) + r0).astype(F32) * (1.0 / (L - 1))
    o_ref[...] = o * jnp.exp(-t * jnp.abs(dec_ref[...]))


def _hyfilt(zf, w1p, b1p, f1p, w2p, b2p, f2p, w3p, dec3, l, L):
    nfc = 4 * MIX
    small = lambda shp: pl.BlockSpec(shp, lambda i: (0,) * len(shp))
    return pl.pallas_call(
        functools.partial(_hyfilt_kernel, L=L), name="hy_filter",
        out_shape=jax.ShapeDtypeStruct((L, nfc), F32),
        grid=(L // HYF_CH,),
        in_specs=[pl.BlockSpec((HYF_CH, 128), lambda i: (i, 0)), small((128, 128)), small((1, 128)), small((1, 128)),
                  small((128, 128)), small((1, 128)), small((1, 128)), small((128, nfc)),
                  pl.BlockSpec((None, 1, nfc), lambda i: (l, 0, 0))],
        out_specs=pl.BlockSpec((HYF_CH, nfc), lambda i: (i, 0)),
        compiler_params=_cp(("arbitrary",), 40),
    )(zf, w1p, b1p, f1p, w2p, b2p, f2p, w3p, dec3)


DFT_SPLIT = 64


def _dft_mats(L):
    N = 2 * L
    a = jnp.arange(L, dtype=jnp.int32)

    def trig(mult, n):
        idx = ((mult * jnp.arange(n, dtype=jnp.int32))[:, None] * a[None, :]) & (N - 1)
        ang = idx.astype(F32) * (2.0 * math.pi / N)
        return jnp.cos(ang), jnp.sin(ang)

    ch, sh = trig(DFT_SPLIT, L // DFT_SPLIT)
    cl, sl = trig(1, DFT_SPLIT)
    cs = (ch[:, None, :] * cl[None] - sh[:, None, :] * sl[None]).reshape(L, L)
    sn = (sh[:, None, :] * cl[None] + ch[:, None, :] * sl[None]).reshape(L, L)
    alt = (1 - 2 * (a & 1)).astype(F32)
    first = (a == 0)
    fs = jnp.where(first[:, None], alt[None, :], -sn)
    F = jnp.stack([cs, fs]).astype(BF16)
    return F, F[1].T


def _kf_kernel(fc_ref, fs_ref, hf_ref, hb_ref, o_ref, hs_s, hd_s, nyq_s):
    @pl.when(pl.program_id(2) == 0)
    def _():
        hf = hf_ref[...]
        rows = lax.broadcasted_iota(jnp.int32, hf.shape, 0)
        hb = jnp.where(rows == 0, 0.0, hb_ref[...])
        hs = hf + hb
        hs_s[...] = hs.astype(BF16)
        hd_s[...] = (hf - hb).astype(BF16)
        alt = (1 - 2 * (rows & 1)).astype(F32)
        nyq_s[...] = jnp.broadcast_to(jnp.sum(hs * alt, axis=0, keepdims=True), nyq_s.shape)

    kr = jnp.dot(fc_ref[...], hs_s[...], preferred_element_type=F32)
    ki = jnp.dot(fs_ref[...], hd_s[...], preferred_element_type=F32)
    bf = kr.shape[0]
    frow = lax.broadcasted_iota(jnp.int32, kr.shape, 0) + pl.program_id(2) * bf
    o_ref[0] = kr
    o_ref[1] = jnp.where(frow == 0, nyq_s[0:1, :], ki)


def _kf(F, hfilt, L):
    bf = min(512, L)
    cw = 512
    ncb = MIX // cw
    return pl.pallas_call(
        _kf_kernel, name="hy_filter_dft",
        out_shape=jax.ShapeDtypeStruct((2, 2, L, MIX), F32),
        grid=(2, ncb, L // bf),
        in_specs=[pl.BlockSpec((None, bf, L), lambda o, c, f: (0, f, 0)),
                  pl.BlockSpec((None, bf, L), lambda o, c, f: (1, f, 0)),
                  pl.BlockSpec((L, cw), lambda o, c, f: (0, o * ncb + c)),
                  pl.BlockSpec((L, cw), lambda o, c, f: (0, (2 + o) * ncb + c))],
        out_specs=pl.BlockSpec((None, 2, bf, cw), lambda o, c, f: (o, 0, f, c)),
        scratch_shapes=[pltpu.VMEM((L, cw), BF16)] * 2 + [pltpu.VMEM((8, cw), F32)],
        compiler_params=_cp(("arbitrary", "arbitrary", "arbitrary"), 48),
    )(F, F, hfilt, hfilt)


def _hyfwd_kernel(z_ref, fc_ref, fs_ref, kr_ref, ki_ref, o_ref):
    zb = z_ref[...]
    zr = jnp.dot(fc_ref[...], zb, preferred_element_type=F32)
    zi = jnp.dot(fs_ref[...], zb, preferred_element_type=F32)
    kr, ki = kr_ref[...], ki_ref[...]
    bf = zr.shape[0]
    frow = lax.broadcasted_iota(jnp.int32, zr.shape, 0) + pl.program_id(0) * bf
    dc = frow == 0
    n_circ = 2 * zb.shape[0]
    wf = jnp.where(dc, 1.0 / n_circ, 2.0 / n_circ)
    o_ref[0] = (jnp.where(dc, zr * kr, zr * kr - zi * ki) * wf).astype(o_ref.dtype)
    o_ref[1] = (jnp.where(dc, zi * ki, zr * ki + zi * kr) * wf).astype(o_ref.dtype)


def _hyfwd(z, zcol, F, KF, o, S, nseq):
    bf = min(512, S)
    return pl.pallas_call(
        _hyfwd_kernel, name="hy_fwd_dft",
        out_shape=jax.ShapeDtypeStruct((nseq, 2, S, MIX), BF16),
        grid=(S // bf, nseq),
        in_specs=[pl.BlockSpec((S, MIX), lambda f, s: (s, zcol)),
                  pl.BlockSpec((None, bf, S), lambda f, s: (0, f, 0)),
                  pl.BlockSpec((None, bf, S), lambda f, s: (1, f, 0)),
                  pl.BlockSpec((None, None, bf, MIX), lambda f, s: (o, 0, f, 0)),
                  pl.BlockSpec((None, None, bf, MIX), lambda f, s: (o, 1, f, 0))],
        out_specs=pl.BlockSpec((None, 2, bf, MIX), lambda f, s: (s, 0, f, 0)),
        compiler_params=_cp(("arbitrary", "arbitrary"), 48),
    )(z, F, F, KF, KF)


def _hyinv_kernel(gc_ref, gs_ref, pr_ref, pi_ref, z_ref, gate_ref, skip_ref, o_ref):
    y = (jnp.dot(gc_ref[...], pr_ref[...], preferred_element_type=F32)
         + jnp.dot(gs_ref[...], pi_ref[...], preferred_element_type=F32))
    y = y + z_ref[...].astype(F32) * skip_ref[...]
    o_ref[...] = (gate_ref[...].astype(F32) * y).astype(o_ref.dtype)


def _hyinv(F, FT, P, z, zcol, u3, gcol, hy_bias, l, o, S, nseq, out_dtype):
    bn = min(512, S)
    nb = S // bn
    return pl.pallas_call(
        _hyinv_kernel, name="hy_inv_dft",
        out_shape=jax.ShapeDtypeStruct((nseq * S, MIX), out_dtype),
        grid=(nseq, nb),
        in_specs=[pl.BlockSpec((None, bn, S), lambda s, n: (0, n, 0)),
                  pl.BlockSpec((bn, S), lambda s, n: (n, 0)),
                  pl.BlockSpec((None, None, S, MIX), lambda s, n: (s, 0, 0, 0)),
                  pl.BlockSpec((None, None, S, MIX), lambda s, n: (s, 1, 0, 0)),
                  pl.BlockSpec((bn, MIX), lambda s, n: (s * nb + n, zcol)),
                  pl.BlockSpec((bn, MIX), lambda s, n: (s * nb + n, gcol)),
                  pl.BlockSpec((None, None, 1, MIX), lambda s, n: (l, o, 0, 0))],
        out_specs=pl.BlockSpec((bn, MIX), lambda s, n: (s * nb + n, 0)),
        compiler_params=_cp(("arbitrary", "arbitrary"), 48),
    )(F, FT, P, P, z, u3, hy_bias)


def _hyena(p, F, FT, KF, hy_short_w, hy_short_b3, hy_bias4, l, S, nseq, row_off):
    u3 = _hyshort(p, hy_short_w, hy_short_b3, l, S, nseq, row_off)
    P = _hyfwd(u3, 0, F, KF, 0, S, nseq)
    z1 = _hyinv(F, FT, P, u3, 0, u3, 1, hy_bias4, l, 0, S, nseq, BF16)
    P = _hyfwd(z1, 0, F, KF, 1, S, nseq)
    return _hyinv(F, FT, P, z1, 0, u3, 2, hy_bias4, l, 1, S, nseq, BF16)


def _rope_tables():
    d_axis = DH // 2
    inv = ROPE_BASE ** (-jnp.arange(0, d_axis, 2, dtype=F32) / d_axis)
    t = jnp.arange(SEQ)
    pos = jnp.stack([t // GRID_W, t % GRID_W], axis=-1).astype(F32)
    lane = np.arange(DH)
    ang = pos[:, lane // d_axis] * inv[lane % (d_axis // 2)][None, :]
    sign = np.where((lane % d_axis) < d_axis // 2, -1.0, 1.0).astype(np.float32)
    return jnp.cos(ang), jnp.sin(ang) * sign[None, :]


def _toeplitz_kernel(r_ref, e_ref, o_ref):
    o_ref[...] = jnp.dot(r_ref[...], e_ref[...], precision=lax.Precision.HIGHEST, preferred_element_type=F32)


def _attn_bias(rpb_l):
    n_dr, n_dc = 2 * WIN_R - 1, 2 * WIN_C - 1
    d = lax.broadcasted_iota(jnp.int32, (128, GRID_W * GRID_W), 0)
    cw = lax.broadcasted_iota(jnp.int32, (128, GRID_W * GRID_W), 1)
    c, w = cw // GRID_W, cw % GRID_W
    c0 = jnp.clip(c - WIN_C // 2, 0, GRID_W - WIN_C)
    col_in = (w >= c0) & (w < c0 + WIN_C)
    sel = jnp.where(col_in, d == jnp.clip(w - c + (WIN_C - 1), 0, n_dc - 1), d == n_dc).astype(F32)
    tab = jnp.concatenate([rpb_l.reshape(HEADS * n_dr, n_dc).astype(F32),
                           jnp.full((HEADS * n_dr, 1), NEG, F32)], axis=1)
    tab = jnp.pad(tab, ((0, 128 - HEADS * n_dr), (0, 128 - n_dc - 1)))
    toe = pl.pallas_call(
        _toeplitz_kernel, name="na_bias_toeplitz",
        out_shape=jax.ShapeDtypeStruct((128, GRID_W * GRID_W), F32),
    )(tab, sel)
    toe = toe[:HEADS * n_dr].reshape(HEADS, n_dr, GRID_W, GRID_W)
    toe = toe.transpose(0, 2, 1, 3).reshape(HEADS, GRID_W, n_dr * GRID_W)
    nblk = GRID_H // ATT_RQ
    out = []
    for blk in (0, 1, nblk - 1):
        start0 = int(np.clip(ATT_RQ * blk - WIN_R // 2, 0, GRID_H - ATT_KR))
        for j in range(ATT_RQ):
            r = ATT_RQ * blk + j
            rs = int(np.clip(r - WIN_R // 2, 0, GRID_H - WIN_R))
            lead, dr0 = rs - start0, rs - r + (WIN_R - 1)
            tail = ATT_KR - WIN_R - lead
            parts = [toe[:, :, dr0 * GRID_W:(dr0 + WIN_R) * GRID_W]]
            if lead:
                parts.insert(0, jnp.full((HEADS, GRID_W, lead * GRID_W), NEG, F32))
            if tail:
                parts.append(jnp.full((HEADS, GRID_W, tail * GRID_W), NEG, F32))
            out.append(jnp.concatenate(parts, axis=-1))
    return jnp.concatenate(out, axis=1).reshape(HEADS, 3, ATT_RQ * GRID_W, ATT_KR * GRID_W)


def _rope(x, cos, sin_signed):
    lane = lax.broadcasted_iota(jnp.int32, x.shape, 1)
    partner = jnp.where((lane & 32) == 0, pltpu.roll(x, DH - 32, axis=1), pltpu.roll(x, 32, axis=1))
    return x * cos + partner * sin_signed


def _attn_kernel(q_ref, k_ref, v_ref, kc_ref, vc_ref, cos_ref, sin_ref, bias_ref, o_ref,
                 qr_s, qp_s, kr_s, v_s):
    scale = DH ** -0.5
    q = q_ref[...].astype(F32) * scale
    cos, sin = cos_ref[...], sin_ref[...]
    qp_s[...] = q.astype(BF16)
    qr_s[...] = _rope(q, cos, sin).astype(BF16)
    kr_s[...] = _rope(k_ref[...].astype(F32), cos, sin).astype(BF16)
    v_s[...] = v_ref[...].astype(BF16)
    kc = kc_ref[...].astype(BF16)
    vc = vc_ref[...].astype(BF16)
    QB, KW = ATT_RQ * GRID_W, ATT_KR * GRID_W
    nblk = GRID_H // ATT_RQ
    tb = (((1,), (1,)), ((), ()))

    def body(i, carry):
        q0 = pl.multiple_of(i * QB, QB)
        start0 = jnp.clip(ATT_RQ * i - WIN_R // 2, 0, GRID_H - ATT_KR)
        k0 = pl.multiple_of(start0 * GRID_W, GRID_W)
        var = jnp.where(i == 0, 0, jnp.where(i == nblk - 1, 2, 1))
        s_loc = lax.dot_general(qr_s[pl.ds(q0, QB), :], kr_s[pl.ds(k0, KW), :], tb,
                                preferred_element_type=F32) + bias_ref[var]
        s_ctx = lax.dot_general(qp_s[pl.ds(q0, QB), :], kc, tb, preferred_element_type=F32)
        m = jnp.maximum(jnp.max(s_loc, axis=-1, keepdims=True), jnp.max(s_ctx, axis=-1, keepdims=True))
        p_loc = jnp.exp(s_loc - m)
        p_ctx = jnp.exp(s_ctx - m)
        den = jnp.sum(p_loc, axis=-1, keepdims=True) + jnp.sum(p_ctx, axis=-1, keepdims=True)
        o = (jnp.dot(p_loc.astype(BF16), v_s[pl.ds(k0, KW), :], preferred_element_type=F32)
             + jnp.dot(p_ctx.astype(BF16), vc, preferred_element_type=F32))
        o_ref[pl.ds(q0, QB), :] = (o / den).astype(o_ref.dtype)
        return carry

    lax.fori_loop(0, nblk, body, 0, unroll=2)


def _attn(p, pc, c_rowblk, c_kcol, c_vcol, cos, sin, bias):
    qc, kc_, vc_ = 3 * MIX // DH, 4 * MIX // DH, 5 * MIX // DH
    lat = lambda c: pl.BlockSpec((SEQ, DH), lambda h, b: (b, c + h))
    ctx = lambda c: pl.BlockSpec((CTX, DH), lambda h, b: (c_rowblk + b, c + h))
    tab = pl.BlockSpec((SEQ, DH), lambda h, b: (0, 0))
    QB, KW = ATT_RQ * GRID_W, ATT_KR * GRID_W
    return pl.pallas_call(
        _attn_kernel, name="na_attention",
        out_shape=jax.ShapeDtypeStruct((N_LAT, MIX), BF16),
        grid=(HEADS, BATCH),
        in_specs=[lat(qc), lat(kc_), lat(vc_), ctx(c_kcol), ctx(c_vcol), tab, tab,
                  pl.BlockSpec((None, 3, QB, KW), lambda h, b: (h, 0, 0, 0))],
        out_specs=pl.BlockSpec((SEQ, DH), lambda h, b: (b, h)),
        scratch_shapes=[pltpu.VMEM((SEQ, DH), BF16)] * 4,
        compiler_params=_cp(("arbitrary", "arbitrary"), 40),
    )(p, p, p, pc, pc, cos, sin, bias)


def _ctxattn_kernel(q_ref, k_ref, v_ref, o_ref):
    scale = DH ** -0.5
    q = (q_ref[...].astype(F32) * scale).astype(BF16)
    s = lax.dot_general(q, k_ref[...].astype(BF16), (((1,), (1,)), ((), ())), preferred_element_type=F32)
    m = jnp.max(s, axis=-1, keepdims=True)
    e = jnp.exp(s - m)
    den = jnp.sum(e, axis=-1, keepdims=True)
    o = jnp.dot(e.astype(BF16), v_ref[...].astype(BF16), preferred_element_type=F32)
    o_ref[...] = (o / den).astype(o_ref.dtype)


def _ctxattn(p):
    rb = N_LAT // CTX
    qc, kc_, vc_ = 3 * MIX // DH, 4 * MIX // DH, 5 * MIX // DH
    blk = lambda c: pl.BlockSpec((CTX, DH), lambda b, h: (rb + b, c + h))
    return pl.pallas_call(
        _ctxattn_kernel, name="ctx_attention",
        out_shape=jax.ShapeDtypeStruct((N_CTX, MIX), BF16),
        grid=(BATCH, HEADS),
        in_specs=[blk(qc), blk(kc_), blk(vc_)],
        out_specs=pl.BlockSpec((CTX, DH), lambda b, h: (b, h)),
        compiler_params=_cp(("arbitrary", "arbitrary"), 32),
    )(p, p, p)


def _pad2(a, r, c):
    return jnp.pad(a, ((0, r - a.shape[0]), (0, c - a.shape[1])))


def kernel(x, c, ctx, c_ctx, ada_w, ada_b, norm_g, ffn1_in, ffn1_out, ffn2_in, ffn2_out, w_in, pool_w, pool_scale, conv_dw_w, conv_dw_b, conv_ln_g, conv_ln_b, na_rpb, hy_short_w, hy_short_b, hy_w1, hy_b1, hy_freq1, hy_w2, hy_b2, hy_freq2, hy_w3, hy_decay, hy_bias, w_branch, w_out):
    X = (x.reshape(N_LAT, D), ctx.reshape(N_CTX, D))
    cc = jnp.concatenate([c, c_ctx[None], jnp.zeros((8 - BATCH - 1, D), F32)], axis=0)
    g3 = norm_g.reshape(DEPTH * 6, 1, D)
    ada_b3 = ada_b.reshape(DEPTH, 1, N_ADA * D)
    vec3 = lambda a: a.reshape(DEPTH, 1, a.shape[-1])
    pool_scale3, conv_dw_b3, conv_ln_g3, conv_ln_b3 = map(vec3, (pool_scale, conv_dw_b, conv_ln_g, conv_ln_b))
    hy_short_b3, hy_decay3 = vec3(hy_short_b), vec3(hy_decay)
    hy_bias4 = hy_bias.reshape(DEPTH, 2, 1, MIX)

    cos, sin = _rope_tables()
    bands = {SEQ: _pool_bands(SEQ), CTX: _pool_bands(CTX)}
    dft = {SEQ: _dft_mats(SEQ), CTX: _dft_mats(CTX)}
    feats = {SEQ: _hy_feats(SEQ), CTX: _hy_feats(CTX)}

    m3 = [_ada(cc, ada_w, ada_b3, l).reshape(8, 1, N_ADA * D) for l in range(DEPTH)]

    rows = N_ALL
    h = _normmod(X[0], X[1], g3, m3[0], 0, 0, 0)
    for l in range(DEPTH):
        last = l == DEPTH - 1
        m = m3[l]

        hid = _ffn_up(h, ffn1_in, l, rows)
        y = _matmul(hid, ffn1_out, l, rows, D, BF16, name="ffn_down", ni=8, bn=512)
        X, h = _resid(X, y, g3, m, l, 1, 2, 0.5, rows, nxt=(g3, m, l, 2, 3))

        if not last:
            p = _matmul(h, w_in, l, rows, PROJ_W, P_DT, name="proj_in", ni=8, bn=1024)
            pc, c_rowblk, c_kcol, c_vcol = p, N_LAT // CTX, 4 * MIX // DH, 5 * MIX // DH
        else:
            p = _matmul(h, w_in, l, N_LAT, PROJ_W, P_DT, name="proj_in", ni=8, bn=1024)
            pc = _matmul(h, w_in, l, N_CTX, 2 * MIX, P_DT, name="proj_ctx_kv", ni=2, bn=1024,
                         row_off=N_LAT, col_off=4 * MIX)
            c_rowblk, c_kcol, c_vcol = 0, 0, MIX // DH

        fpar = (_pad2(hy_w1[l], 128, 128), _pad2(hy_b1[l][None], 1, 128), _pad2(hy_freq1[l][None], 1, 128),
                _pad2(hy_w2[l], 128, 128), _pad2(hy_b2[l][None], 1, 128), _pad2(hy_freq2[l][None], 1, 128),
                _pad2(hy_w3[l], 128, 4 * MIX))

        def mixers(S, nseq, row_off):
            F, G = dft[S]
            KF = _kf(F, _hyfilt(feats[S], *fpar, hy_decay3, l, S), S)
            y0 = _pool(p, bands[S], pool_w, pool_scale3, l, S, nseq, row_off)
            y1 = _conv(p, conv_dw_w, conv_dw_b3, conv_ln_g3, conv_ln_b3, l, S, nseq, row_off)
            y3 = _hyena(p, F, G, KF, hy_short_w, hy_short_b3, hy_bias4, l, S, nseq, row_off)
            return y0, y1, y3

        y0, y1, y3 = mixers(SEQ, BATCH, 0)
        y2 = _attn(p, pc, c_rowblk, c_kcol, c_vcol, cos, sin, _attn_bias(na_rpb[l]))
        if not last:
            c0, c1, c3 = mixers(CTX, BATCH, N_LAT)
            ys_ctx = [c0, c1, _ctxattn(p), c3]
            mrows = rows
        else:
            ys_ctx = None
            mrows = N_LAT
        merged = _branch([y0, y1, y2, y3], ys_ctx, p, w_branch, l, mrows)
        y = _matmul(merged, w_out, l, mrows, D, BF16, name="proj_out", ni=8, bn=1024)
        X, h = _resid(X, y, g3, m, l, 3, 5, 1.0, mrows, nxt=(g3, m, l, 4, 6))
        rows = mrows

        hid = _ffn_up(h, ffn2_in, l, rows)
        y = _matmul(hid, ffn2_out, l, rows, D, BF16, name="ffn_down", ni=8, bn=512)
        if not last:
            X, h = _resid(X, y, g3, m, l, 5, 8, 0.5, rows, nxt=(g3, m3[l + 1], l + 1, 0, 0))
        else:
            X = _resid(X, y, g3, m, l, 5, 8, 0.5, rows)
    return X.reshape(BATCH, SEQ, D)
```
